```python
import math
import jax, jax.numpy as jnp
from jax import lax
import numpy as np


D_MODEL = 1024
BATCH = 16
SEQ = 2048
DEPTH = 1

GRID_W = 64
CTX_LEN = 256
MLA_HEADS = 8
MLA_NOPE = 64
MLA_ROPE = 32
MLA_V = 64
Q_LORA = 512
KV_LORA = 256
QBLK = 128
ROPE_BASE = 10000.0
DN_HEADS = 4
DN_DK = 128
DN_DV = 128
DN_QK = DN_HEADS * DN_DK
DN_VW = DN_HEADS * DN_DV
CONV_W = 3
CHUNK = 64
FF_HIDDEN = -(-8 * D_MODEL // (3 * 256)) * 256
EPS = 1e-6
IN_SIZES = (Q_LORA, KV_LORA, MLA_ROPE, 2 * DN_QK + DN_VW, DN_VW, 2 * DN_HEADS, 2 * DN_HEADS, D_MODEL, D_MODEL)
IN_COLS = sum(IN_SIZES)

kernel_name = 'hybrid_mla_gdn_flow_block'


def rmsnorm(x, g):
    xf = x.astype(jnp.float32)
    y = xf * lax.rsqrt(jnp.mean(xf * xf, axis=-1, keepdims=True) + EPS)
    return (y * g.astype(jnp.float32)).astype(x.dtype)


def l2norm(a):
    af = a.astype(jnp.float32)
    return (af * lax.rsqrt(jnp.sum(af * af, axis=-1, keepdims=True) + EPS)).astype(a.dtype)


def modulate(h, shift, scale):
    return h * (1.0 + scale) + shift


def split_cols(p):
    out = []
    start = 0
    for n in IN_SIZES:
        out.append(p[..., start:start + n])
        start += n
    return out


def axial_rope_tables(rows):
    row = jnp.repeat(jnp.arange(rows, dtype=jnp.float32), GRID_W)
    col = jnp.tile(jnp.arange(GRID_W, dtype=jnp.float32), rows)
    n_freq = MLA_ROPE // 4
    inv = ROPE_BASE ** (-jnp.arange(n_freq, dtype=jnp.float32) / n_freq)
    ang = jnp.concatenate([row[:, None] * inv, col[:, None] * inv], axis=-1)
    return jnp.cos(ang), jnp.sin(ang)


def apply_axial_rope(x, cos, sin):
    B, T, Hh, _ = x.shape
    n_freq = MLA_ROPE // 4
    xr = x.reshape(B, T, Hh, 2, 2, n_freq)
    cs = cos.reshape(T, 1, 2, n_freq).astype(x.dtype)
    sn = sin.reshape(T, 1, 2, n_freq).astype(x.dtype)
    x1, x2 = xr[..., 0, :], xr[..., 1, :]
    out = jnp.stack([x1 * cs - x2 * sn, x1 * sn + x2 * cs], axis=-2)
    return out.reshape(B, T, Hh, MLA_ROPE)


def mla_query(cq, g_q_lora, w_uq, cos, sin):
    B, T, _ = cq.shape
    q = (rmsnorm(cq, g_q_lora) @ w_uq).reshape(B, T, MLA_HEADS, MLA_NOPE + MLA_ROPE)
    if cos is None:
        return q
    return jnp.concatenate([q[..., :MLA_NOPE], apply_axial_rope(q[..., MLA_NOPE:], cos, sin)], axis=-1)


def mla_key_value(ckv, kr, g_kv_lora, w_ukv, cos, sin):
    B, T, _ = ckv.shape
    kv = (rmsnorm(ckv, g_kv_lora) @ w_ukv).reshape(B, T, MLA_HEADS, MLA_NOPE + MLA_V)
    k_rope = kr[:, :, None, :]
    if cos is not None:
        k_rope = apply_axial_rope(k_rope, cos, sin)
    k = jnp.concatenate([kv[..., :MLA_NOPE], jnp.broadcast_to(k_rope, (B, T, MLA_HEADS, MLA_ROPE))], axis=-1)
    return k, kv[..., MLA_NOPE:]


def attend(q, k, v):
    B, T, H, Dqk = q.shape
    nb = T // QBLK
    scale = Dqk ** -0.5
    qb = q.reshape(B, nb, QBLK, H, Dqk).swapaxes(0, 1)

    def block(qi):
        s = jnp.einsum('bqhd,bkhd->bhqk', qi, k).astype(jnp.float32) * scale
        p = jax.nn.softmax(s, axis=-1).astype(v.dtype)
        return jnp.einsum('bhqk,bkhd->bqhd', p, v)

    o = lax.map(block, qb)
    return o.swapaxes(0, 1).reshape(B, T, H, v.shape[-1])


def short_conv(u, w):
    T = u.shape[1]
    pad = CONV_W // 2
    up = jnp.pad(u, ((0, 0), (pad, pad), (0, 0)))
    out = up[:, 0:T] * w[0]
    for i in range(1, CONV_W):
        out = out + up[:, i:i + T] * w[i]
    return out


def deltanet_prep(qkv_raw, beta_raw, alpha_raw, conv_w, a_log, dt_bias):
    B, T, _ = qkv_raw.shape
    qkv = jax.nn.silu(short_conv(qkv_raw, conv_w))
    q = l2norm(qkv[..., :DN_QK].reshape(B, T, DN_HEADS, DN_DK))
    k = l2norm(qkv[..., DN_QK:2 * DN_QK].reshape(B, T, DN_HEADS, DN_DK))
    v = qkv[..., 2 * DN_QK:].reshape(B, T, DN_HEADS, DN_DV)
    beta = jax.nn.sigmoid(beta_raw.astype(jnp.float32)).reshape(B, T, 2, DN_HEADS)
    g = -jnp.exp(a_log.astype(jnp.float32)) * jax.nn.softplus(
        alpha_raw.astype(jnp.float32).reshape(B, T, 2, DN_HEADS) + dt_bias.astype(jnp.float32))
    return q, k, v, beta, g


def chunk_gated_delta(q, k, v, g, beta, s0, emit_out):
    out_dtype = v.dtype
    B, T, H, DK = k.shape
    DV = v.shape[-1]
    N = T // CHUNK
    f = jnp.float32

    def chunks(a):
        return a.astype(f).reshape(B, N, CHUNK, H, -1).transpose(1, 0, 3, 2, 4)

    kc, vc = chunks(k), chunks(v)
    gc = jnp.cumsum(chunks(g[..., None])[..., 0], axis=-1)
    bc = chunks(beta[..., None])[..., 0]
    idx = jnp.arange(CHUNK)
    incl = idx[:, None] >= idx[None, :]
    strict = idx[:, None] > idx[None, :]
    decay = jnp.exp(jnp.where(incl, gc[..., :, None] - gc[..., None, :], -jnp.inf))
    kk = jnp.einsum('nbhcd,nbhed->nbhce', kc, kc)
    a_mat = jnp.where(strict, kk * bc[..., :, None] * decay, 0.0) + jnp.eye(CHUNK, dtype=f)
    rhs = jnp.concatenate([vc * bc[..., None], kc * (bc * jnp.exp(gc))[..., None]], axis=-1)
    sol = lax.linalg.triangular_solve(a_mat, rhs, left_side=True, lower=True, unit_diagonal=True)
    u, w = sol[..., :DV], sol[..., DV:]
    if emit_out:
        qc = chunks(q) * DK ** -0.5
        qk = jnp.einsum('nbhcd,nbhed->nbhce', qc, kc) * decay
        xs = (kc, u, w, gc, qc, qk)
    else:
        xs = (kc, u, w, gc)

    def step(S, xs_i):
        k_i, u_i, w_i, g_i = xs_i[:4]
        v_new = u_i - jnp.einsum('bhcd,bhde->bhce', w_i, S)
        g_last = g_i[..., -1]
        S_next = S * jnp.exp(g_last)[..., None, None] + jnp.einsum(
            'bhcd,bhce->bhde', k_i * jnp.exp(g_last[..., None] - g_i)[..., None], v_new)
        if emit_out:
            q_i, qk_i = xs_i[4], xs_i[5]
            o = jnp.einsum('bhcd,bhde->bhce', q_i * jnp.exp(g_i)[..., None], S) + jnp.einsum(
                'bhce,bhef->bhcf', qk_i, v_new)
            return S_next, o
        return S_next, None

    s_final, o = lax.scan(step, s0.astype(f), xs)
    if not emit_out:
        return None, s_final
    o = o.transpose(1, 0, 3, 2, 4).reshape(B, T, H, DV).astype(out_dtype)
    return o, s_final


def flip(a):
    return jnp.flip(a, axis=1)


def gated_rmsnorm(o, z, g):
    B, T = o.shape[:2]
    y = rmsnorm(o, g) * jax.nn.silu(z.reshape(o.shape))
    return y.reshape(B, T, -1)


def merge_branches(o_mla, o_dn, gate_a, gate_b, w_o_mla, w_o_dn, w_out):
    B, T = o_mla.shape[:2]
    y_a = o_mla.reshape(B, T, -1) @ w_o_mla
    y_b = o_dn @ w_o_dn
    return (jax.nn.sigmoid(gate_a) * y_a + jax.nn.sigmoid(gate_b) * y_b) @ w_out


def token_mixer(u, uc, cos, sin, with_ctx_out, w_in, g_q_lora, w_uq, g_kv_lora, w_ukv, w_o_mla,
                conv_qkv, a_log, dt_bias, g_dn_out, w_o_dn, w_out):
    cq, ckv, kr, qkv_raw, z, beta_raw, alpha_raw, gate_a, gate_b = split_cols(u @ w_in)
    cq_c, ckv_c, kr_c, qkv_c, z_c, beta_c, alpha_c, gate_a_c, gate_b_c = split_cols(uc @ w_in)
    q = mla_query(cq, g_q_lora, w_uq, cos, sin)
    k, v = mla_key_value(ckv, kr, g_kv_lora, w_ukv, cos, sin)
    k_c, v_c = mla_key_value(ckv_c, kr_c, g_kv_lora, w_ukv, None, None)
    o_mla = attend(q, jnp.concatenate([k, k_c], axis=1), jnp.concatenate([v, v_c], axis=1))
    ql, kl, vl, bl, gl = deltanet_prep(qkv_raw, beta_raw, alpha_raw, conv_qkv, a_log, dt_bias)
    qx, kx, vx, bx, gx = deltanet_prep(qkv_c, beta_c, alpha_c, conv_qkv, a_log, dt_bias)
    s0 = jnp.zeros((u.shape[0], DN_HEADS, DN_DK, DN_DV), jnp.float32)
    of_c, sf_c = chunk_gated_delta(qx, kx, vx, gx[:, :, 0], bx[:, :, 0], s0, with_ctx_out)
    ob_c, sb_c = chunk_gated_delta(flip(qx), flip(kx), flip(vx), flip(gx[:, :, 1]), flip(bx[:, :, 1]), s0, with_ctx_out)
    of_l, _ = chunk_gated_delta(ql, kl, vl, gl[:, :, 0], bl[:, :, 0], sf_c, True)
    ob_l, _ = chunk_gated_delta(flip(ql), flip(kl), flip(vl), flip(gl[:, :, 1]), flip(bl[:, :, 1]), sb_c, True)
    o_dn = gated_rmsnorm(of_l + flip(ob_l), z, g_dn_out)
    y = merge_branches(o_mla, o_dn, gate_a, gate_b, w_o_mla, w_o_dn, w_out)
    if not with_ctx_out:
        return y, None
    q_c = mla_query(cq_c, g_q_lora, w_uq, None, None)
    o_mla_c = attend(q_c, k_c, v_c)
    o_dn_c = gated_rmsnorm(of_c + flip(ob_c), z_c, g_dn_out)
    y_c = merge_branches(o_mla_c, o_dn_c, gate_a_c, gate_b_c, w_o_mla, w_o_dn, w_out)
    return y, y_c


def swiglu(u, w_in, w_out):
    gu = u @ w_in
    return (jax.nn.silu(gu[..., :FF_HIDDEN]) * gu[..., FF_HIDDEN:]) @ w_out


def setup_inputs(seed: int = 0) -> dict:
    key = jax.random.key(seed)
    ks = jax.random.split(key, 26)
    L = DEPTH
    f = jnp.float32

    def nrm(k, shape, fan_in, s=1.0):
        return jax.random.normal(k, shape, f) * (s * fan_in ** -0.5)

    def gain(k, n):
        return 1.0 + 0.1 * jax.random.normal(k, (L, n), f)

    dt = jnp.exp(jax.random.uniform(ks[20], (L, 2, DN_HEADS), f, minval=math.log(1e-3), maxval=math.log(1e-1)))
    return {
        'x': jax.random.normal(ks[0], (BATCH, SEQ, D_MODEL), f),
        'c': jax.random.normal(ks[1], (BATCH, D_MODEL), f),
        'ctx': jax.random.normal(ks[2], (BATCH, CTX_LEN, D_MODEL), f),
        'c_ctx': jax.random.normal(ks[3], (D_MODEL,), f),
        'w_mod': nrm(ks[4], (L, D_MODEL, 6 * D_MODEL), D_MODEL, 0.5),
        'b_mod': 0.02 * jax.random.normal(ks[5], (L, 6 * D_MODEL), f),
        'g_pre_mix': gain(ks[6], D_MODEL),
        'g_post_mix': gain(ks[7], D_MODEL),
        'g_pre_ffn': gain(ks[8], D_MODEL),
        'g_post_ffn': gain(ks[9], D_MODEL),
        'w_in': nrm(ks[10], (L, D_MODEL, IN_COLS), D_MODEL),
        'g_q_lora': gain(ks[11], Q_LORA),
        'w_uq': nrm(ks[12], (L, Q_LORA, MLA_HEADS * (MLA_NOPE + MLA_ROPE)), Q_LORA),
        'g_kv_lora': gain(ks[13], KV_LORA),
        'w_ukv': nrm(ks[14], (L, KV_LORA, MLA_HEADS * (MLA_NOPE + MLA_V)), KV_LORA),
        'w_o_mla': nrm(ks[15], (L, MLA_HEADS * MLA_V, D_MODEL), MLA_HEADS * MLA_V),
        'conv_qkv': nrm(ks[16], (L, CONV_W, 2 * DN_QK + DN_VW), CONV_W),
        'a_log': jnp.log(jax.random.uniform(ks[17], (L, 2, DN_HEADS), f, minval=1.0, maxval=16.0)),
        'dt_bias': dt + jnp.log(-jnp.expm1(-dt)),
        'g_dn_out': gain(ks[18], DN_DV),
        'w_o_dn': nrm(ks[19], (L, DN_VW, D_MODEL), DN_VW),
        'w_out': nrm(ks[21], (L, D_MODEL, D_MODEL), D_MODEL),
        'w_ffn_in': nrm(ks[22], (L, D_MODEL, 2 * FF_HIDDEN), D_MODEL),
        'w_ffn_out': nrm(ks[23], (L, FF_HIDDEN, D_MODEL), FF_HIDDEN),
    }


def reference(x, c, ctx, c_ctx, w_mod, b_mod, g_pre_mix, g_post_mix, g_pre_ffn, g_post_ffn, w_in,
              g_q_lora, w_uq, g_kv_lora, w_ukv, w_o_mla, conv_qkv, a_log, dt_bias, g_dn_out, w_o_dn,
              w_out, w_ffn_in, w_ffn_out):
    rows = x.shape[1] // GRID_W
    cos, sin = axial_rope_tables(rows)
    h_c = ctx
    for layer in range(DEPTH):
        ctx_out = layer < DEPTH - 1
        mod = jax.nn.silu(c) @ w_mod[layer] + b_mod[layer]
        mod_c = jax.nn.silu(c_ctx) @ w_mod[layer] + b_mod[layer]
        sh1, sc1, gt1, sh2, sc2, gt2 = jnp.split(mod[:, None, :], 6, axis=-1)
        sh1c, sc1c, gt1c, sh2c, sc2c, gt2c = jnp.split(mod_c, 6)
        u = modulate(rmsnorm(x, g_pre_mix[layer]), sh1, sc1)
        uc = modulate(rmsnorm(h_c, g_pre_mix[layer]), sh1c, sc1c)
        y, y_c = token_mixer(u, uc, cos, sin, ctx_out, w_in[layer], g_q_lora[layer], w_uq[layer],
                             g_kv_lora[layer], w_ukv[layer], w_o_mla[layer], conv_qkv[layer], a_log[layer],
                             dt_bias[layer], g_dn_out[layer], w_o_dn[layer], w_out[layer])
        x = x + gt1 * rmsnorm(y, g_post_mix[layer])
        u = modulate(rmsnorm(x, g_pre_ffn[layer]), sh2, sc2)
        x = x + gt2 * rmsnorm(swiglu(u, w_ffn_in[layer], w_ffn_out[layer]), g_post_ffn[layer])
        if ctx_out:
            h_c = h_c + gt1c * rmsnorm(y_c, g_post_mix[layer])
            uc = modulate(rmsnorm(h_c, g_pre_ffn[layer]), sh2c, sc2c)
            h_c = h_c + gt2c * rmsnorm(swiglu(uc, w_ffn_in[layer], w_ffn_out[layer]), g_post_ffn[layer])
    return x
```

```python
import functools
import math

import jax
import jax.numpy as jnp
from jax import lax
from jax.experimental import pallas as pl
from jax.experimental.pallas import tpu as pltpu

F32 = jnp.float32
BF16 = jnp.bfloat16

D_MODEL = 1024
GRID_W = 64
MLA_HEADS = 8
MLA_NOPE = 64
MLA_ROPE = 32
MLA_V = 64
Q_LORA = 512
KV_LORA = 256
ROPE_BASE = 10000.0
DN_HEADS = 4
DN_DK = 128
DN_DV = 128
DN_QK = DN_HEADS * DN_DK
DN_VW = DN_HEADS * DN_DV
FF_HIDDEN = 2816
EPS = 1e-6

LANES = 128
DN_CHUNK = 128
ROW_TILE = 256
Q_TILE = 256
FF_TILE = 256
PACK_ROWS = 16
VMEM_LIMIT = 56 * 1024 * 1024
LOG2E = math.log2(math.e)

C_CQ = 0
C_CKV = C_CQ + Q_LORA
C_M1 = C_CKV + KV_LORA
C_M2 = C_M1 + LANES
C_QKV = C_M2 + LANES
C_Z = C_QKV + 2 * DN_QK + DN_VW
C_GATE = C_Z + DN_VW
C_END = C_GATE + 2 * D_MODEL
C_CTX_END = C_Z


def _mm(a, b):
    return jnp.dot(a, b, preferred_element_type=F32)


def _mm_nt(a, b):
    return lax.dot_general(a, b, (((1,), (1,)), ((), ())), preferred_element_type=F32)


def _sigmoid(x):
    return 1.0 / (1.0 + jnp.exp(-x))


def _silu(x):
    return x * _sigmoid(x)


def _rms(x, g):
    return x * lax.rsqrt(jnp.mean(x * x, axis=-1, keepdims=True) + EPS) * g


def _params(n_axes):
    return pltpu.CompilerParams(dimension_semantics=("arbitrary",) * n_axes, vmem_limit_bytes=VMEM_LIMIT)


def _const_spec(shape):
    nd = len(shape)
    return pl.BlockSpec(shape, lambda *_: (0,) * nd, pipeline_mode=pl.Buffered(1))


def _mod_body(c_ref, w_ref, b_ref, o_ref):
    s = _silu(c_ref[...])
    o_ref[...] = jnp.dot(s, w_ref[...], preferred_element_type=F32, precision=lax.Precision.HIGHEST) + b_ref[...]


def _modulation(cc, w_mod, b_mod):
    rows, d = cc.shape
    n = w_mod.shape[1]
    tn = 1024
    return pl.pallas_call(
        _mod_body,
        grid=(n // tn,),
        in_specs=[pl.BlockSpec((rows, d), lambda j: (0, 0)),
                  pl.BlockSpec((d, tn), lambda j: (0, j)),
                  pl.BlockSpec((1, tn), lambda j: (0, j))],
        out_specs=pl.BlockSpec((rows, tn), lambda j: (0, j)),
        out_shape=jax.ShapeDtypeStruct((rows, n), F32),
        compiler_params=_params(1),
        name="modulation",
    )(cc, w_mod, b_mod.reshape(1, n))


def _inproj_body(x_ref, sh_ref, sc_ref, gpre_ref, w_ref, gq_ref, wq1_ref, wq2_ref, gkv_ref, wk_ref, wv_ref,
                 cq_tab, sq_tab, ck_tab, sk_tab, *outs, full):
    if full:
        q_ref, k_ref, v_ref, qkv_ref, ba_ref, z_ref, gate_ref = outs
    else:
        k_ref, v_ref, qkv_ref, ba_ref = outs
    u = (_rms(x_ref[...], gpre_ref[...]) * (1.0 + sc_ref[...]) + sh_ref[...]).astype(BF16)

    if full:
        nq = _rms(_mm(u, w_ref[:, C_CQ:C_CKV]), gq_ref[...]).astype(BF16)
        qa = _mm(nq, wq1_ref[...])
        qb = _mm(nq, wq2_ref[...])
        cq = cq_tab[...]
        sq = sq_tab[...]
        for h in range(MLA_HEADS):
            sl = slice(h * LANES, (h + 1) * LANES)
            q_ref[:, sl] = (qa[:, sl] * cq + qb[:, sl] * sq).astype(BF16)

    nkv = _rms(_mm(u, w_ref[:, C_CKV:C_M1]), gkv_ref[...]).astype(BF16)
    m1 = _mm(u, w_ref[:, C_M1:C_M2])
    m2 = _mm(u, w_ref[:, C_M2:C_QKV])
    ba_ref[...] = m1
    k_rope = m1 * ck_tab[...] + m2 * sk_tab[...]
    kn = _mm(nkv, wk_ref[...])
    for h in range(MLA_HEADS):
        sl = slice(h * LANES, (h + 1) * LANES)
        k_ref[:, sl] = (kn[:, sl] + k_rope).astype(BF16)
    v_ref[...] = _mm(nkv, wv_ref[...]).astype(BF16)
    qkv_ref[...] = _mm(u, w_ref[:, C_QKV:C_Z]).astype(BF16)
    if full:
        z_ref[...] = _mm(u, w_ref[:, C_Z:C_GATE]).astype(BF16)
        gate_ref[...] = _sigmoid(_mm(u, w_ref[:, C_GATE:C_END])).astype(BF16)


def _input_projection(x2, mod3, mod_period, tab_period, w, gpre, gq, wq1, wq2, gkv, wk, wv, tabs, full):
    n, d = x2.shape
    tm = min(ROW_TILE, tab_period)
    mod_tiles = mod_period // tm
    tab_tiles = tab_period // tm
    row = lambda i: (i, 0)
    tab = lambda i: (i % tab_tiles, 0)
    in_specs = [
        pl.BlockSpec((tm, d), row),
        pl.BlockSpec((None, 1, d), lambda i: (i // mod_tiles, 0, 0)),
        pl.BlockSpec((None, 1, d), lambda i: (i // mod_tiles, 0, 1)),
        _const_spec(gpre.shape), _const_spec(w.shape), _const_spec(gq.shape), _const_spec(wq1.shape),
        _const_spec(wq2.shape), _const_spec(gkv.shape), _const_spec(wk.shape), _const_spec(wv.shape),
        pl.BlockSpec((tm, LANES), tab), pl.BlockSpec((tm, LANES), tab), pl.BlockSpec((tm, LANES), tab),
        pl.BlockSpec((tm, LANES), tab),
    ]
    kw = MLA_HEADS * LANES
    vw = MLA_HEADS * MLA_V
    qkvw = 2 * DN_QK + DN_VW
    outs = [(kw, BF16), (vw, BF16), (qkvw, BF16), (LANES, F32)]
    if full:
        outs = [(kw, BF16)] + outs + [(DN_VW, BF16), (2 * D_MODEL, BF16)]
    return pl.pallas_call(
        functools.partial(_inproj_body, full=full),
        grid=(n // tm,),
        in_specs=in_specs,
        out_specs=[pl.BlockSpec((tm, c), row) for c, _ in outs],
        out_shape=[jax.ShapeDtypeStruct((n, c), t) for c, t in outs],
        compiler_params=_params(1),
        name="input_projection" if full else "input_projection_ctx",
    )(x2, mod3, mod3, gpre, w, gq, wq1, wq2, gkv, wk, wv, *tabs)


def _attn_body(q_ref, kl_ref, kc_ref, vl_ref, vc_ref, o_ref):
    for h in range(MLA_HEADS):
        sl = slice(h * LANES, (h + 1) * LANES)
        sv = slice(h * MLA_V, (h + 1) * MLA_V)
        qh = q_ref[:, sl]
        s1 = _mm_nt(qh, kl_ref[:, sl])
        s2 = _mm_nt(qh, kc_ref[:, sl])
        m = jnp.maximum(jnp.max(s1, axis=-1, keepdims=True), jnp.max(s2, axis=-1, keepdims=True))
        p1 = jnp.exp2(s1 - m)
        p2 = jnp.exp2(s2 - m)
        l = jnp.sum(p1, axis=-1, keepdims=True) + jnp.sum(p2, axis=-1, keepdims=True)
        o = _mm(p1.astype(BF16), vl_ref[:, sv]) + _mm(p2.astype(BF16), vc_ref[:, sv])
        o_ref[:, sv] = (o / l).astype(BF16)


def _attention(q, kl, kc, vl, vc):
    b, t, kw = q.shape
    s = kc.shape[1]
    vw = vl.shape[2]
    tq = min(Q_TILE, t)
    return pl.pallas_call(
        _attn_body,
        grid=(b, t // tq),
        in_specs=[pl.BlockSpec((None, tq, kw), lambda i, j: (i, j, 0)),
                  pl.BlockSpec((None, t, kw), lambda i, j: (i, 0, 0)),
                  pl.BlockSpec((None, s, kw), lambda i, j: (i, 0, 0)),
                  pl.BlockSpec((None, t, vw), lambda i, j: (i, 0, 0)),
                  pl.BlockSpec((None, s, vw), lambda i, j: (i, 0, 0))],
        out_specs=pl.BlockSpec((None, tq, vw), lambda i, j: (i, j, 0)),
        out_shape=jax.ShapeDtypeStruct((b, t, vw), BF16),
        compiler_params=_params(2),
        name="attention",
    )(q, kl, kc, vl, vc)


def _dn_prep_body(x_ref, xp_ref, xn_ref, ba_ref, cw_ref, alog_ref, dtb_ref, *outs, nchunks, emit):
    if emit:
        u_ref, w_ref, kdt_ref, dl_ref, qg_ref, qkm_ref = outs
    else:
        u_ref, w_ref, kdt_ref, dl_ref = outs
    c = pl.program_id(1)
    C = DN_CHUNK
    has_prev = jnp.where(c > 0, 1.0, 0.0)
    has_next = jnp.where(c < nchunks - 1, 1.0, 0.0)
    rowi = lax.broadcasted_iota(jnp.int32, (C, C), 0)
    coli = lax.broadcasted_iota(jnp.int32, (C, C), 1)

    def conv_slab(s):
        sl = slice(s * LANES, (s + 1) * LANES)
        x = x_ref[:, sl].astype(F32)
        prev = xp_ref[PACK_ROWS - 1:PACK_ROWS, sl].astype(F32) * has_prev
        nxt = xn_ref[0:1, sl].astype(F32) * has_next
        xd = jnp.where(rowi == 0, prev, pltpu.roll(x, 1, 0))
        xu = jnp.where(rowi == C - 1, nxt, pltpu.roll(x, C - 1, 0))
        return _silu(xd * cw_ref[0:1, sl] + x * cw_ref[1:2, sl] + xu * cw_ref[2:3, sl])

    def l2n(a):
        return a * lax.rsqrt(jnp.sum(a * a, axis=-1, keepdims=True) + EPS)

    bg = ba_ref[...]
    beta = _sigmoid(bg)
    xs = bg + dtb_ref[...]
    softplus = jnp.maximum(xs, 0.0) + jnp.log(1.0 + jnp.exp(-jnp.abs(xs)))
    g = -jnp.exp(alog_ref[...]) * softplus
    gc = g
    sft = 1
    while sft < C:
        gc = gc + jnp.where(rowi >= sft, pltpu.roll(gc, sft, 0), 0.0)
        sft *= 2
    gtot = gc[C - 1:C, :]
    gcr = gtot - gc + g
    gc_t = gc.T
    gcr_t = gcr.T
    e_gc = jnp.exp(gc)
    e_gcr = jnp.exp(gcr)
    e_rem = jnp.exp(gtot - gc)
    e_rem_r = jnp.exp(gtot - gcr)
    dl_all = jnp.exp(jnp.broadcast_to(gc_t[8:16, C - 1:C], (8, LANES)))
    scale = DN_DK ** -0.5

    for h in range(DN_HEADS):
        hs = slice(h * LANES, (h + 1) * LANES)
        qh = l2n(conv_slab(h)) * scale
        kh = l2n(conv_slab(DN_HEADS + h))
        vh = conv_slab(2 * DN_HEADS + h)
        khb = kh.astype(BF16)
        kk = _mm_nt(khb, khb)
        if emit:
            qk = _mm_nt(qh.astype(BF16), khb)
        for d in range(2):
            lane = 8 + 4 * d + h
            gsrc, gsrc_t, esrc, erem = (gc, gc_t, e_gc, e_rem) if d == 0 else (gcr, gcr_t, e_gcr, e_rem_r)
            gcol = gsrc[:, lane:lane + 1]
            grow = gsrc_t[lane:lane + 1, :]
            bcol = beta[:, 4 * d + h:4 * d + h + 1]
            incl = (rowi >= coli) if d == 0 else (rowi <= coli)
            strict = (rowi > coli) if d == 0 else (rowi < coli)
            decay = jnp.exp(jnp.where(incl, gcol - grow, -1e30))
            nmat = jnp.where(strict, -(kk * bcol) * decay, 0.0)
            r = jnp.where((rowi >> 1) == (coli >> 1), nmat, 0.0)
            for lg in range(1, int(math.log2(C))):
                ns = jnp.where(((rowi >> (lg + 1)) == (coli >> (lg + 1))) & ((rowi >> lg) != (coli >> lg)), nmat, 0.0)
                rb = r.astype(BF16)
                m = ns + _mm(rb, ns.astype(BF16))
                r = r + m + _mm(m.astype(BF16), rb)
            ecol = esrc[:, lane:lane + 1]
            rhs = jnp.concatenate([vh * bcol, kh * (bcol * ecol)], axis=1)
            uw = rhs + _mm(r.astype(BF16), rhs.astype(BF16))
            u_ref[d, :, hs] = uw[:, :LANES].astype(BF16)
            w_ref[d, :, hs] = uw[:, LANES:].astype(BF16)
            kdt_ref[d, :, hs] = (kh * erem[:, lane:lane + 1]).T.astype(BF16)
            dl_ref[d, h:h + 1, :] = dl_all[4 * d + h:4 * d + h + 1, :]
            if emit:
                qg_ref[d, :, hs] = (qh * ecol).astype(BF16)
                qkm_ref[d, :, hs] = jnp.where(incl, qk * decay, 0.0).astype(BF16)


def _dn_prepare(qkv, ba, conv_w, alog, dtb, emit):
    b, t, cw = qkv.shape
    C = DN_CHUNK
    nchunks = t // C
    per = C // PACK_ROWS
    last = t // PACK_ROWS - 1
    seq_out = pl.BlockSpec((None, 2, C, DN_VW), lambda i, c: (i, 0, c, 0))
    seq_shape = jax.ShapeDtypeStruct((b, 2, t, DN_VW), BF16)
    out_specs = [seq_out, seq_out, seq_out, pl.BlockSpec((None, 2, None, DN_HEADS, LANES), lambda i, c: (i, 0, c, 0, 0))]
    out_shape = [seq_shape, seq_shape, seq_shape, jax.ShapeDtypeStruct((b, 2, nchunks, DN_HEADS, LANES), F32)]
    if emit:
        out_specs += [seq_out, seq_out]
        out_shape += [seq_shape, seq_shape]
    return pl.pallas_call(
        functools.partial(_dn_prep_body, nchunks=nchunks, emit=emit),
        grid=(b, nchunks),
        in_specs=[pl.BlockSpec((None, C, cw), lambda i, c: (i, c, 0)),
                  pl.BlockSpec((None, PACK_ROWS, cw), lambda i, c: (i, jnp.maximum(c * per - 1, 0), 0)),
                  pl.BlockSpec((None, PACK_ROWS, cw), lambda i, c: (i, jnp.minimum((c + 1) * per, last), 0)),
                  pl.BlockSpec((None, C, LANES), lambda i, c: (i, c, 0)),
                  _const_spec(conv_w.shape), _const_spec(alog.shape), _const_spec(dtb.shape)],
        out_specs=out_specs,
        out_shape=out_shape,
        compiler_params=_params(2),
        name="deltanet_prepare" if emit else "deltanet_prepare_ctx",
    )(qkv, qkv, qkv, ba, conv_w, alog, dtb)


def _dn_scan_body(uc_ref, wc_ref, kdc_ref, dlc_ref, ul_ref, wl_ref, kdl_ref, dll_ref, qg_ref, qkm_ref, o_ref, s_ref,
                  *, n_ctx, n_lat):
    d = pl.program_id(1)
    C = DN_CHUNK
    s_ref[...] = jnp.zeros_like(s_ref)

    def step(c, n, u_ref, w_ref, kdt_ref, dl_ref, emit):
        ce = jnp.where(d == 0, c, n - 1 - c)
        r = pl.multiple_of(ce * C, C)
        for h in range(DN_HEADS):
            hs = slice(h * LANES, (h + 1) * LANES)
            s = s_ref[h]
            sb = s.astype(BF16)
            vn = u_ref[pl.ds(r, C), hs].astype(F32) - _mm(w_ref[pl.ds(r, C), hs], sb)
            vnb = vn.astype(BF16)
            if emit:
                o = _mm(qg_ref[pl.ds(r, C), hs], sb) + _mm(qkm_ref[pl.ds(r, C), hs], vnb)
                o_ref[pl.ds(r, C), hs] = o.astype(BF16)
            s_ref[h] = s * dl_ref[ce, h:h + 1, :] + _mm(kdt_ref[pl.ds(r, C), hs], vnb)

    for c in range(n_ctx):
        step(c, n_ctx, uc_ref, wc_ref, kdc_ref, dlc_ref, False)

    def body(c, carry):
        step(c, n_lat, ul_ref, wl_ref, kdl_ref, dll_ref, True)
        return carry

    lax.fori_loop(0, n_lat, body, 0)


def _dn_scan(ctx_ops, lat_ops):
    uc, wc, kdc, dlc = ctx_ops
    ul, wl, kdl, dll, qg, qkm = lat_ops
    b, _, t, vw = ul.shape
    s = uc.shape[2]
    n_ctx, n_lat = s // DN_CHUNK, t // DN_CHUNK
    seq = lambda n: pl.BlockSpec((None, None, n, vw), lambda i, d: (i, d, 0, 0))
    dls = lambda n: pl.BlockSpec((None, None, n, DN_HEADS, LANES), lambda i, d: (i, d, 0, 0, 0))
    return pl.pallas_call(
        functools.partial(_dn_scan_body, n_ctx=n_ctx, n_lat=n_lat),
        grid=(b, 2),
        in_specs=[seq(s), seq(s), seq(s), dls(n_ctx), seq(t), seq(t), seq(t), dls(n_lat), seq(t), seq(t)],
        out_specs=seq(t),
        out_shape=jax.ShapeDtypeStruct((b, 2, t, vw), BF16),
        scratch_shapes=[pltpu.VMEM((DN_HEADS, DN_DK, DN_DV), F32)],
        compiler_params=_params(2),
        name="deltanet_scan",
    )(uc, wc, kdc, dlc, ul, wl, kdl, dll, qg, qkm)


def _merge_ffn_body(x_ref, om_ref, of_ref, ob_ref, z_ref, gate_ref, gt1_ref, sh2_ref, sc2_ref, gt2_ref,
                    gdn_ref, woa_ref, wod_ref, wout_ref, gpost_ref, gpre2_ref, w1_ref, w2_ref, gpost2_ref, out_ref):
    odn = of_ref[...].astype(F32) + ob_ref[...].astype(F32)
    parts = []
    for h in range(DN_HEADS):
        hs = slice(h * DN_DV, (h + 1) * DN_DV)
        parts.append(_rms(odn[:, hs], gdn_ref[...]) * _silu(z_ref[:, hs].astype(F32)))
    o_dn = jnp.concatenate(parts, axis=1).astype(BF16)
    ya = _mm(om_ref[...], woa_ref[...])
    yb = _mm(o_dn, wod_ref[...])
    mix = gate_ref[:, :D_MODEL].astype(F32) * ya + gate_ref[:, D_MODEL:].astype(F32) * yb
    y = _mm(mix.astype(BF16), wout_ref[...])
    x1 = x_ref[...] + gt1_ref[...] * _rms(y, gpost_ref[...])
    u2 = (_rms(x1, gpre2_ref[...]) * (1.0 + sc2_ref[...]) + sh2_ref[...]).astype(BF16)
    acc = jnp.zeros(x1.shape, F32)
    for j in range(FF_HIDDEN // FF_TILE):
        a = _mm(u2, w1_ref[:, j * FF_TILE:(j + 1) * FF_TILE])
        up = _mm(u2, w1_ref[:, FF_HIDDEN + j * FF_TILE:FF_HIDDEN + (j + 1) * FF_TILE])
        acc = acc + _mm((_silu(a) * up).astype(BF16), w2_ref[j * FF_TILE:(j + 1) * FF_TILE, :])
    out_ref[...] = x1 + gt2_ref[...] * _rms(acc, gpost2_ref[...])


def _merge_ffn(x2, seq, om, o_dn2, z, gates, mod3, gdn, woa, wod, wout, gpost, gpre2, w1, w2, gpost2):
    n, d = x2.shape
    tm = min(ROW_TILE, seq)
    tiles = seq // tm
    nb = n // seq
    row = lambda i: (i, 0)
    modspec = lambda k: pl.BlockSpec((None, 1, d), lambda i: (i // tiles, 0, k))
    dn_spec = lambda dirn: pl.BlockSpec((None, None, tm, DN_VW), lambda i: (i // tiles, dirn, i % tiles, 0))
    return pl.pallas_call(
        _merge_ffn_body,
        grid=(n // tm,),
        in_specs=[pl.BlockSpec((tm, d), row), pl.BlockSpec((tm, om.shape[1]), row), dn_spec(0), dn_spec(1),
                  pl.BlockSpec((tm, DN_VW), row), pl.BlockSpec((tm, 2 * d), row),
                  modspec(2), modspec(3), modspec(4), modspec(5),
                  _const_spec(gdn.shape), _const_spec(woa.shape), _const_spec(wod.shape), _const_spec(wout.shape),
                  _const_spec(gpost.shape), _const_spec(gpre2.shape), _const_spec(w1.shape), _const_spec(w2.shape),
                  _const_spec(gpost2.shape)],
        out_specs=pl.BlockSpec((tm, d), row),
        out_shape=jax.ShapeDtypeStruct((n, d), F32),
        compiler_params=_params(1),
        name="merge_ffn",
    )(x2, om, o_dn2, o_dn2, z, gates, mod3, mod3, mod3, mod3, gdn, woa, wod, wout, gpost, gpre2, w1, w2, gpost2)


def _rot_half(w):
    n = MLA_ROPE // 4
    return jnp.concatenate([-w[..., n:2 * n], w[..., 0:n], -w[..., 3 * n:4 * n], w[..., 2 * n:3 * n]], axis=-1)


def _rope_tables(rows):
    row = jnp.repeat(jnp.arange(rows, dtype=F32), GRID_W)
    col = jnp.tile(jnp.arange(GRID_W, dtype=F32), rows)
    n = MLA_ROPE // 4
    inv = ROPE_BASE ** (-jnp.arange(n, dtype=F32) / n)
    ar, ac = row[:, None] * inv, col[:, None] * inv
    cos = jnp.concatenate([jnp.cos(ar), jnp.cos(ar), jnp.cos(ac), jnp.cos(ac)], axis=-1)
    sin = jnp.concatenate([jnp.sin(ar), jnp.sin(ar), jnp.sin(ac), jnp.sin(ac)], axis=-1)
    return cos, sin


def _pad_tab(nope_val, rope, t):
    return jnp.concatenate([jnp.full((t, MLA_NOPE), nope_val, F32), rope,
                            jnp.zeros((t, LANES - MLA_NOPE - MLA_ROPE), F32)], axis=-1)


def kernel(*args):
    return _forward(*args)[0]


def kernel_stages(*args):
    return _forward(*args)[1]


def _forward(x, c, ctx, c_ctx, w_mod, b_mod, g_pre_mix, g_post_mix, g_pre_ffn, g_post_ffn, w_in, g_q_lora, w_uq,
             g_kv_lora, w_ukv, w_o_mla, conv_qkv, a_log, dt_bias, g_dn_out, w_o_dn, w_out, w_ffn_in, w_ffn_out):
    B, T, D = x.shape
    S = ctx.shape[1]
    assert w_mod.shape[0] == 1 and D == D_MODEL
    assert T % max(ROW_TILE, DN_CHUNK, Q_TILE) == 0 and S % DN_CHUNK == 0 and T % GRID_W == 0
    L = 0
    row2 = lambda a: a[L].reshape(1, -1)

    mod_rows = -(-(B + 1) // 8) * 8
    cc = jnp.concatenate([c, c_ctx[None, :], jnp.zeros((mod_rows - B - 1, D), F32)], axis=0)
    mod3 = _modulation(cc, w_mod[L], b_mod[L]).reshape(mod_rows, 1, 6 * D)

    wi = w_in[L]
    o_cq, o_ckv, o_kr = 0, Q_LORA, Q_LORA + KV_LORA
    o_qkv = o_kr + MLA_ROPE
    o_z = o_qkv + 2 * DN_QK + DN_VW
    o_beta = o_z + DN_VW
    o_gate = o_beta + 4 * DN_HEADS
    w_kr = wi[:, o_kr:o_qkv]
    zc = lambda n: jnp.zeros((D, n), F32)
    w_pack = jnp.concatenate([
        wi[:, o_cq:o_kr],
        wi[:, o_beta:o_gate], zc(MLA_NOPE - 4 * DN_HEADS), w_kr, zc(LANES - MLA_NOPE - MLA_ROPE),
        zc(MLA_NOPE), _rot_half(w_kr), zc(LANES - MLA_NOPE - MLA_ROPE),
        wi[:, o_qkv:o_beta], wi[:, o_gate:]], axis=1).astype(BF16)
    assert w_pack.shape[1] == C_END

    dq = MLA_NOPE + MLA_ROPE
    wq = w_uq[L].reshape(Q_LORA, MLA_HEADS, dq)
    zq = jnp.zeros((Q_LORA, MLA_HEADS, LANES - dq), F32)
    wq1 = jnp.concatenate([wq, zq], axis=-1).reshape(Q_LORA, MLA_HEADS * LANES).astype(BF16)
    wq2 = jnp.concatenate([jnp.zeros((Q_LORA, MLA_HEADS, MLA_NOPE), F32), _rot_half(wq[..., MLA_NOPE:]), zq],
                          axis=-1).reshape(Q_LORA, MLA_HEADS * LANES).astype(BF16)
    wkv = w_ukv[L].reshape(KV_LORA, MLA_HEADS, MLA_NOPE + MLA_V)
    wk = jnp.concatenate([wkv[..., :MLA_NOPE], jnp.zeros((KV_LORA, MLA_HEADS, LANES - MLA_NOPE), F32)],
                         axis=-1).reshape(KV_LORA, MLA_HEADS * LANES).astype(BF16)
    wv = wkv[..., MLA_NOPE:].reshape(KV_LORA, MLA_HEADS * MLA_V).astype(BF16)

    cos, sin = _rope_tables(T // GRID_W)
    qs = dq ** -0.5 * LOG2E
    tabs_lat = (_pad_tab(qs, cos * qs, T), _pad_tab(0.0, sin * qs, T), _pad_tab(0.0, cos, T), _pad_tab(0.0, sin, T))
    ones = jnp.ones((S, MLA_ROPE), F32)
    no_rot = _pad_tab(0.0, 0.0 * ones, S)
    tabs_ctx = (no_rot, no_rot, _pad_tab(0.0, ones, S), no_rot)

    x2 = x.reshape(B * T, D)
    c2 = ctx.reshape(B * S, D)
    gpre, gq, gkv = row2(g_pre_mix), row2(g_q_lora), row2(g_kv_lora)
    q, k, v, qkv, ba, z, gates = _input_projection(
        x2, mod3, T, T, w_pack, gpre, gq, wq1, wq2, gkv, wk, wv, tabs_lat, True)
    k_c, v_c, qkv_c, ba_c = _input_projection(
        c2, mod3[B:B + 1], B * S, S, w_pack[:, :C_CTX_END], gpre, gq, wq1, wq2, gkv, wk, wv, tabs_ctx, False)

    r3 = lambda a, n: a.reshape(B, n, a.shape[-1])
    o_mla = _attention(r3(q, T), r3(k, T), r3(k_c, S), r3(v, T), r3(v_c, S))

    lane_pad = lambda a, fill: jnp.pad(a.astype(F32).reshape(1, -1), ((0, 0), (8, LANES - 16)), constant_values=fill)
    alog = lane_pad(a_log[L], -1e30)
    dtb = lane_pad(dt_bias[L], 0.0)
    cw = conv_qkv[L].astype(F32)
    ctx_ops = _dn_prepare(r3(qkv_c, S), r3(ba_c, S), cw, alog, dtb, False)
    lat_ops = _dn_prepare(r3(qkv, T), r3(ba, T), cw, alog, dtb, True)
    o_dn2 = _dn_scan(ctx_ops, lat_ops)

    out = _merge_ffn(x2, T, o_mla.reshape(B * T, -1), o_dn2, z, gates, mod3, row2(g_dn_out),
                     w_o_mla[L].astype(BF16), w_o_dn[L].astype(BF16), w_out[L].astype(BF16), row2(g_post_mix),
                     row2(g_pre_ffn), w_ffn_in[L].astype(BF16), w_ffn_out[L].astype(BF16), row2(g_post_ffn))
    stages = dict(mod=mod3.reshape(mod_rows, -1), q=q, k=k, v=v, qkv=qkv, ba=ba, z=z, gates=gates, k_c=k_c, v_c=v_c,
                  qkv_c=qkv_c, ba_c=ba_c, o_mla=o_mla, o_dn2=o_dn2)
    return out.reshape(B, T, D), stages
```

```python
import functools
import math

import jax
import jax.numpy as jnp
from jax import lax
from jax.experimental import pallas as pl
from jax.experimental.pallas import tpu as pltpu

F32 = jnp.float32
BF16 = jnp.bfloat16

D_MODEL = 1024
GRID_W = 64
MLA_HEADS = 8
MLA_NOPE = 64
MLA_ROPE = 32
MLA_V = 64
Q_LORA = 512
KV_LORA = 256
ROPE_BASE = 10000.0
DN_HEADS = 4
DN_DK = 128
DN_DV = 128
DN_QK = DN_HEADS * DN_DK
DN_VW = DN_HEADS * DN_DV
FF_HIDDEN = 2816
EPS = 1e-6

LANES = 128
DN_CHUNK = 128
ROW_TILE = 256
Q_TILE = 256
FF_TILE = 256
PACK_ROWS = 16
VMEM_LIMIT = 56 * 1024 * 1024
LOG2E = math.log2(math.e)

C_CQ = 0
C_CKV = C_CQ + Q_LORA
C_M1 = C_CKV + KV_LORA
C_M2 = C_M1 + LANES
C_QKV = C_M2 + LANES
C_Z = C_QKV + 2 * DN_QK + DN_VW
C_GATE = C_Z + DN_VW
C_END = C_GATE + 2 * D_MODEL
C_CTX_END = C_Z


def _mm(a, b):
    return jnp.dot(a, b, preferred_element_type=F32)


def _mm_nt(a, b):
    return lax.dot_general(a, b, (((1,), (1,)), ((), ())), preferred_element_type=F32)


def _sigmoid(x):
    return 1.0 / (1.0 + jnp.exp(-x))


def _silu(x):
    return x * _sigmoid(x)


def _rms(x, g):
    return x * lax.rsqrt(jnp.mean(x * x, axis=-1, keepdims=True) + EPS) * g


def _params(n_axes):
    return pltpu.CompilerParams(dimension_semantics=("arbitrary",) * n_axes, vmem_limit_bytes=VMEM_LIMIT)


def _const_spec(shape):
    nd = len(shape)
    return pl.BlockSpec(shape, lambda *_: (0,) * nd, pipeline_mode=pl.Buffered(1))


def _mod_body(c_ref, w_ref, b_ref, o_ref):
    s = _silu(c_ref[...])
    o_ref[...] = jnp.dot(s, w_ref[...], preferred_element_type=F32, precision=lax.Precision.HIGHEST) + b_ref[...]


def _modulation(cc, w_mod, b_mod):
    rows, d = cc.shape
    n = w_mod.shape[1]
    tn = 1024
    return pl.pallas_call(
        _mod_body,
        grid=(n // tn,),
        in_specs=[pl.BlockSpec((rows, d), lambda j: (0, 0)),
                  pl.BlockSpec((d, tn), lambda j: (0, j)),
                  pl.BlockSpec((1, tn), lambda j: (0, j))],
        out_specs=pl.BlockSpec((rows, tn), lambda j: (0, j)),
        out_shape=jax.ShapeDtypeStruct((rows, n), F32),
        compiler_params=_params(1),
        name="modulation",
    )(cc, w_mod, b_mod.reshape(1, n))


def _inproj_body(x_ref, sh_ref, sc_ref, gpre_ref, w_ref, gq_ref, wq1_ref, wq2_ref, gkv_ref, wk_ref, wv_ref,
                 cq_tab, sq_tab, ck_tab, sk_tab, *outs, full):
    if full:
        q_ref, k_ref, v_ref, qkv_ref, ba_ref, z_ref, gate_ref = outs
    else:
        k_ref, v_ref, qkv_ref, ba_ref = outs
    u = (_rms(x_ref[...], gpre_ref[...]) * (1.0 + sc_ref[...]) + sh_ref[...]).astype(BF16)

    if full:
        nq = _rms(_mm(u, w_ref[:, C_CQ:C_CKV]), gq_ref[...]).astype(BF16)
        qa = _mm(nq, wq1_ref[...])
        qb = _mm(nq, wq2_ref[...])
        cq = cq_tab[...]
        sq = sq_tab[...]
        for h in range(MLA_HEADS):
            sl = slice(h * LANES, (h + 1) * LANES)
            q_ref[:, sl] = (qa[:, sl] * cq + qb[:, sl] * sq).astype(BF16)

    nkv = _rms(_mm(u, w_ref[:, C_CKV:C_M1]), gkv_ref[...]).astype(BF16)
    m1 = _mm(u, w_ref[:, C_M1:C_M2])
    m2 = _mm(u, w_ref[:, C_M2:C_QKV])
    ba_ref[...] = m1
    k_rope = m1 * ck_tab[...] + m2 * sk_tab[...]
    kn = _mm(nkv, wk_ref[...])
    for h in range(MLA_HEADS):
        sl = slice(h * LANES, (h + 1) * LANES)
        k_ref[:, sl] = (kn[:, sl] + k_rope).astype(BF16)
    v_ref[...] = _mm(nkv, wv_ref[...]).astype(BF16)
    qkv_ref[...] = _mm(u, w_ref[:, C_QKV:C_Z]).astype(BF16)
    if full:
        z_ref[...] = _mm(u, w_ref[:, C_Z:C_GATE]).astype(BF16)
        gate_ref[...] = _sigmoid(_mm(u, w_ref[:, C_GATE:C_END])).astype(BF16)


def _input_projection(x2, mod3, mod_period, tab_period, w, gpre, gq, wq1, wq2, gkv, wk, wv, tabs, full):
    n, d = x2.shape
    tm = min(ROW_TILE, tab_period)
    mod_tiles = mod_period // tm
    tab_tiles = tab_period // tm
    row = lambda i: (i, 0)
    tab = lambda i: (i % tab_tiles, 0)
    in_specs = [
        pl.BlockSpec((tm, d), row),
        pl.BlockSpec((None, 1, d), lambda i: (i // mod_tiles, 0, 0)),
        pl.BlockSpec((None, 1, d), lambda i: (i // mod_tiles, 0, 1)),
        _const_spec(gpre.shape), _const_spec(w.shape), _const_spec(gq.shape), _const_spec(wq1.shape),
        _const_spec(wq2.shape), _const_spec(gkv.shape), _const_spec(wk.shape), _const_spec(wv.shape),
        pl.BlockSpec((tm, LANES), tab), pl.BlockSpec((tm, LANES), tab), pl.BlockSpec((tm, LANES), tab),
        pl.BlockSpec((tm, LANES), tab),
    ]
    kw = MLA_HEADS * LANES
    vw = MLA_HEADS * MLA_V
    qkvw = 2 * DN_QK + DN_VW
    outs = [(kw, BF16), (vw, BF16), (qkvw, BF16), (LANES, F32)]
    if full:
        outs = [(kw, BF16)] + outs + [(DN_VW, BF16), (2 * D_MODEL, BF16)]
    return pl.pallas_call(
        functools.partial(_inproj_body, full=full),
        grid=(n // tm,),
        in_specs=in_specs,
        out_specs=[pl.BlockSpec((tm, c), row) for c, _ in outs],
        out_shape=[jax.ShapeDtypeStruct((n, c), t) for c, t in outs],
        compiler_params=_params(1),
        name="input_projection" if full else "input_projection_ctx",
    )(x2, mod3, mod3, gpre, w, gq, wq1, wq2, gkv, wk, wv, *tabs)


def _attn_body(q_ref, kl_ref, kc_ref, vl_ref, vc_ref, o_ref):
    for h in range(MLA_HEADS):
        sl = slice(h * LANES, (h + 1) * LANES)
        sv = slice(h * MLA_V, (h + 1) * MLA_V)
        qh = q_ref[:, sl]
        s1 = _mm_nt(qh, kl_ref[:, sl])
        s2 = _mm_nt(qh, kc_ref[:, sl])
        m = jnp.maximum(jnp.max(s1, axis=-1, keepdims=True), jnp.max(s2, axis=-1, keepdims=True))
        p1 = jnp.exp2(s1 - m)
        p2 = jnp.exp2(s2 - m)
        l = jnp.sum(p1, axis=-1, keepdims=True) + jnp.sum(p2, axis=-1, keepdims=True)
        o = _mm(p1.astype(BF16), vl_ref[:, sv]) + _mm(p2.astype(BF16), vc_ref[:, sv])
        o_ref[:, sv] = (o / l).astype(BF16)


def _attention(q, kl, kc, vl, vc):
    b, t, kw = q.shape
    s = kc.shape[1]
    vw = vl.shape[2]
    tq = min(Q_TILE, t)
    return pl.pallas_call(
        _attn_body,
        grid=(b, t // tq),
        in_specs=[pl.BlockSpec((None, tq, kw), lambda i, j: (i, j, 0)),
                  pl.BlockSpec((None, t, kw), lambda i, j: (i, 0, 0)),
                  pl.BlockSpec((None, s, kw), lambda i, j: (i, 0, 0)),
                  pl.BlockSpec((None, t, vw), lambda i, j: (i, 0, 0)),
                  pl.BlockSpec((None, s, vw), lambda i, j: (i, 0, 0))],
        out_specs=pl.BlockSpec((None, tq, vw), lambda i, j: (i, j, 0)),
        out_shape=jax.ShapeDtypeStruct((b, t, vw), BF16),
        compiler_params=_params(2),
        name="attention",
    )(q, kl, kc, vl, vc)


def _dn_prep_body(x_ref, xp_ref, xn_ref, ba_ref, cw_ref, alog_ref, dtb_ref, *outs, nchunks, emit):
    if emit:
        u_ref, w_ref, kdt_ref, dl_ref, qg_ref, qkm_ref = outs
    else:
        u_ref, w_ref, kdt_ref, dl_ref = outs
    c = pl.program_id(1)
    C = DN_CHUNK
    has_prev = jnp.where(c > 0, 1.0, 0.0)
    has_next = jnp.where(c < nchunks - 1, 1.0, 0.0)
    rowi = lax.broadcasted_iota(jnp.int32, (C, C), 0)
    coli = lax.broadcasted_iota(jnp.int32, (C, C), 1)

    def conv_slab(s):
        sl = slice(s * LANES, (s + 1) * LANES)
        x = x_ref[:, sl].astype(F32)
        prev = xp_ref[PACK_ROWS - 1:PACK_ROWS, sl].astype(F32) * has_prev
        nxt = xn_ref[0:1, sl].astype(F32) * has_next
        xd = jnp.where(rowi == 0, prev, pltpu.roll(x, 1, 0))
        xu = jnp.where(rowi == C - 1, nxt, pltpu.roll(x, C - 1, 0))
        return _silu(xd * cw_ref[0:1, sl] + x * cw_ref[1:2, sl] + xu * cw_ref[2:3, sl])

    def l2n(a):
        return a * lax.rsqrt(jnp.sum(a * a, axis=-1, keepdims=True) + EPS)

    bg = ba_ref[...]
    beta = _sigmoid(bg)
    xs = bg + dtb_ref[...]
    softplus = jnp.maximum(xs, 0.0) + jnp.log(1.0 + jnp.exp(-jnp.abs(xs)))
    g = -jnp.exp(alog_ref[...]) * softplus
    gc = g
    sft = 1
    while sft < C:
        gc = gc + jnp.where(rowi >= sft, pltpu.roll(gc, sft, 0), 0.0)
        sft *= 2
    gtot = gc[C - 1:C, :]
    gcr = gtot - gc + g
    gc_t = gc.T
    gcr_t = gcr.T
    e_gc = jnp.exp(gc)
    e_gcr = jnp.exp(gcr)
    e_rem = jnp.exp(gtot - gc)
    e_rem_r = jnp.exp(gtot - gcr)
    dl_all = jnp.exp(jnp.broadcast_to(gc_t[8:16, C - 1:C], (8, LANES)))
    scale = DN_DK ** -0.5

    systems = []
    for h in range(DN_HEADS):
        hs = slice(h * LANES, (h + 1) * LANES)
        qh = l2n(conv_slab(h)) * scale
        kh = l2n(conv_slab(DN_HEADS + h))
        vh = conv_slab(2 * DN_HEADS + h)
        khb = kh.astype(BF16)
        kk = _mm_nt(khb, khb)
        if emit:
            qk = _mm_nt(qh.astype(BF16), khb)
        for d in range(2):
            lane = 8 + 4 * d + h
            gsrc, gsrc_t, esrc, erem = (gc, gc_t, e_gc, e_rem) if d == 0 else (gcr, gcr_t, e_gcr, e_rem_r)
            gcol = gsrc[:, lane:lane + 1]
            grow = gsrc_t[lane:lane + 1, :]
            bcol = beta[:, 4 * d + h:4 * d + h + 1]
            incl = (rowi >= coli) if d == 0 else (rowi <= coli)
            strict = (rowi > coli) if d == 0 else (rowi < coli)
            decay = jnp.exp(jnp.where(incl, gcol - grow, -1e30))
            ecol = esrc[:, lane:lane + 1]
            systems.append((d, hs, jnp.where(strict, -(kk * bcol) * decay, 0.0),
                            jnp.concatenate([vh * bcol, kh * (bcol * ecol)], axis=1)))
            kdt_ref[d, :, hs] = (kh * erem[:, lane:lane + 1]).T.astype(BF16)
            dl_ref[d, h:h + 1, :] = dl_all[4 * d + h:4 * d + h + 1, :]
            if emit:
                qg_ref[d, :, hs] = (qh * ecol).astype(BF16)
                qkm_ref[d, :, hs] = jnp.where(incl, qk * decay, 0.0).astype(BF16)

    rs = [jnp.where((rowi >> 1) == (coli >> 1), nmat, 0.0) for _, _, nmat, _ in systems]
    for lg in range(1, int(math.log2(C))):
        cross = ((rowi >> (lg + 1)) == (coli >> (lg + 1))) & ((rowi >> lg) != (coli >> lg))
        rbs = [r.astype(BF16) for r in rs]
        ms = []
        for (_, _, nmat, _), rb in zip(systems, rbs):
            ns = jnp.where(cross, nmat, 0.0)
            ms.append(ns + _mm(rb, ns.astype(BF16)))
        rs = [r + m + _mm(m.astype(BF16), rb) for r, m, rb in zip(rs, ms, rbs)]
    for (d, hs, _, rhs), r in zip(systems, rs):
        uw = rhs + _mm(r.astype(BF16), rhs.astype(BF16))
        u_ref[d, :, hs] = uw[:, :LANES].astype(BF16)
        w_ref[d, :, hs] = uw[:, LANES:].astype(BF16)


def _dn_prepare(qkv, ba, conv_w, alog, dtb, emit):
    b, t, cw = qkv.shape
    C = DN_CHUNK
    nchunks = t // C
    per = C // PACK_ROWS
    last = t // PACK_ROWS - 1
    seq_out = pl.BlockSpec((None, 2, C, DN_VW), lambda i, c: (i, 0, c, 0))
    seq_shape = jax.ShapeDtypeStruct((b, 2, t, DN_VW), BF16)
    out_specs = [seq_out, seq_out, seq_out, pl.BlockSpec((None, 2, None, DN_HEADS, LANES), lambda i, c: (i, 0, c, 0, 0))]
    out_shape = [seq_shape, seq_shape, seq_shape, jax.ShapeDtypeStruct((b, 2, nchunks, DN_HEADS, LANES), F32)]
    if emit:
        out_specs += [seq_out, seq_out]
        out_shape += [seq_shape, seq_shape]
    return pl.pallas_call(
        functools.partial(_dn_prep_body, nchunks=nchunks, emit=emit),
        grid=(b, nchunks),
        in_specs=[pl.BlockSpec((None, C, cw), lambda i, c: (i, c, 0)),
                  pl.BlockSpec((None, PACK_ROWS, cw), lambda i, c: (i, jnp.maximum(c * per - 1, 0), 0)),
                  pl.BlockSpec((None, PACK_ROWS, cw), lambda i, c: (i, jnp.minimum((c + 1) * per, last), 0)),
                  pl.BlockSpec((None, C, LANES), lambda i, c: (i, c, 0)),
                  _const_spec(conv_w.shape), _const_spec(alog.shape), _const_spec(dtb.shape)],
        out_specs=out_specs,
        out_shape=out_shape,
        compiler_params=_params(2),
        name="deltanet_prepare" if emit else "deltanet_prepare_ctx",
    )(qkv, qkv, qkv, ba, conv_w, alog, dtb)


def _dn_scan_body(uc_ref, wc_ref, kdc_ref, dlc_ref, ul_ref, wl_ref, kdl_ref, dll_ref, qg_ref, qkm_ref, o_ref, s_ref,
                  *, n_ctx, n_lat):
    d = pl.program_id(1)
    C = DN_CHUNK
    s_ref[...] = jnp.zeros_like(s_ref)

    def step(c, n, u_ref, w_ref, kdt_ref, dl_ref, emit):
        ce = jnp.where(d == 0, c, n - 1 - c)
        r = pl.multiple_of(ce * C, C)
        for h in range(DN_HEADS):
            hs = slice(h * LANES, (h + 1) * LANES)
            s = s_ref[h]
            sb = s.astype(BF16)
            vn = u_ref[pl.ds(r, C), hs].astype(F32) - _mm(w_ref[pl.ds(r, C), hs], sb)
            vnb = vn.astype(BF16)
            if emit:
                o = _mm(qg_ref[pl.ds(r, C), hs], sb) + _mm(qkm_ref[pl.ds(r, C), hs], vnb)
                o_ref[pl.ds(r, C), hs] = o.astype(BF16)
            s_ref[h] = s * dl_ref[ce, h:h + 1, :] + _mm(kdt_ref[pl.ds(r, C), hs], vnb)

    for c in range(n_ctx):
        step(c, n_ctx, uc_ref, wc_ref, kdc_ref, dlc_ref, False)

    def body(c, carry):
        step(c, n_lat, ul_ref, wl_ref, kdl_ref, dll_ref, True)
        return carry

    lax.fori_loop(0, n_lat, body, 0)


def _dn_scan(ctx_ops, lat_ops):
    uc, wc, kdc, dlc = ctx_ops
    ul, wl, kdl, dll, qg, qkm = lat_ops
    b, _, t, vw = ul.shape
    s = uc.shape[2]
    n_ctx, n_lat = s // DN_CHUNK, t // DN_CHUNK
    seq = lambda n: pl.BlockSpec((None, None, n, vw), lambda i, d: (i, d, 0, 0))
    dls = lambda n: pl.BlockSpec((None, None, n, DN_HEADS, LANES), lambda i, d: (i, d, 0, 0, 0))
    return pl.pallas_call(
        functools.partial(_dn_scan_body, n_ctx=n_ctx, n_lat=n_lat),
        grid=(b, 2),
        in_specs=[seq(s), seq(s), seq(s), dls(n_ctx), seq(t), seq(t), seq(t), dls(n_lat), seq(t), seq(t)],
        out_specs=seq(t),
        out_shape=jax.ShapeDtypeStruct((b, 2, t, vw), BF16),
        scratch_shapes=[pltpu.VMEM((DN_HEADS, DN_DK, DN_DV), F32)],
        compiler_params=_params(2),
        name="deltanet_scan",
    )(uc, wc, kdc, dlc, ul, wl, kdl, dll, qg, qkm)


def _merge_ffn_body(x_ref, om_ref, of_ref, ob_ref, z_ref, gate_ref, gt1_ref, sh2_ref, sc2_ref, gt2_ref,
                    gdn_ref, woa_ref, wod_ref, wout_ref, gpost_ref, gpre2_ref, w1_ref, w2_ref, gpost2_ref, out_ref):
    odn = of_ref[...].astype(F32) + ob_ref[...].astype(F32)
    parts = []
    for h in range(DN_HEADS):
        hs = slice(h * DN_DV, (h + 1) * DN_DV)
        parts.append(_rms(odn[:, hs], gdn_ref[...]) * _silu(z_ref[:, hs].astype(F32)))
    o_dn = jnp.concatenate(parts, axis=1).astype(BF16)
    ya = _mm(om_ref[...], woa_ref[...])
    yb = _mm(o_dn, wod_ref[...])
    mix = gate_ref[:, :D_MODEL].astype(F32) * ya + gate_ref[:, D_MODEL:].astype(F32) * yb
    y = _mm(mix.astype(BF16), wout_ref[...])
    x1 = x_ref[...] + gt1_ref[...] * _rms(y, gpost_ref[...])
    u2 = (_rms(x1, gpre2_ref[...]) * (1.0 + sc2_ref[...]) + sh2_ref[...]).astype(BF16)
    acc = jnp.zeros(x1.shape, F32)
    for j in range(FF_HIDDEN // FF_TILE):
        a = _mm(u2, w1_ref[:, j * FF_TILE:(j + 1) * FF_TILE])
        up = _mm(u2, w1_ref[:, FF_HIDDEN + j * FF_TILE:FF_HIDDEN + (j + 1) * FF_TILE])
        acc = acc + _mm((_silu(a) * up).astype(BF16), w2_ref[j * FF_TILE:(j + 1) * FF_TILE, :])
    out_ref[...] = x1 + gt2_ref[...] * _rms(acc, gpost2_ref[...])


def _merge_ffn(x2, seq, om, o_dn2, z, gates, mod3, gdn, woa, wod, wout, gpost, gpre2, w1, w2, gpost2):
    n, d = x2.shape
    tm = min(ROW_TILE, seq)
    tiles = seq // tm
    nb = n // seq
    row = lambda i: (i, 0)
    modspec = lambda k: pl.BlockSpec((None, 1, d), lambda i: (i // tiles, 0, k))
    dn_spec = lambda dirn: pl.BlockSpec((None, None, tm, DN_VW), lambda i: (i // tiles, dirn, i % tiles, 0))
    return pl.pallas_call(
        _merge_ffn_body,
        grid=(n // tm,),
        in_specs=[pl.BlockSpec((tm, d), row), pl.BlockSpec((tm, om.shape[1]), row), dn_spec(0), dn_spec(1),
                  pl.BlockSpec((tm, DN_VW), row), pl.BlockSpec((tm, 2 * d), row),
                  modspec(2), modspec(3), modspec(4), modspec(5),
                  _const_spec(gdn.shape), _const_spec(woa.shape), _const_spec(wod.shape), _const_spec(wout.shape),
                  _const_spec(gpost.shape), _const_spec(gpre2.shape), _const_spec(w1.shape), _const_spec(w2.shape),
                  _const_spec(gpost2.shape)],
        out_specs=pl.BlockSpec((tm, d), row),
        out_shape=jax.ShapeDtypeStruct((n, d), F32),
        compiler_params=_params(1),
        name="merge_ffn",
    )(x2, om, o_dn2, o_dn2, z, gates, mod3, mod3, mod3, mod3, gdn, woa, wod, wout, gpost, gpre2, w1, w2, gpost2)


def _rot_half(w):
    n = MLA_ROPE // 4
    return jnp.concatenate([-w[..., n:2 * n], w[..., 0:n], -w[..., 3 * n:4 * n], w[..., 2 * n:3 * n]], axis=-1)


def _rope_tables(rows):
    row = jnp.repeat(jnp.arange(rows, dtype=F32), GRID_W)
    col = jnp.tile(jnp.arange(GRID_W, dtype=F32), rows)
    n = MLA_ROPE // 4
    inv = ROPE_BASE ** (-jnp.arange(n, dtype=F32) / n)
    ar, ac = row[:, None] * inv, col[:, None] * inv
    cos = jnp.concatenate([jnp.cos(ar), jnp.cos(ar), jnp.cos(ac), jnp.cos(ac)], axis=-1)
    sin = jnp.concatenate([jnp.sin(ar), jnp.sin(ar), jnp.sin(ac), jnp.sin(ac)], axis=-1)
    return cos, sin


def _pad_tab(nope_val, rope, t):
    return jnp.concatenate([jnp.full((t, MLA_NOPE), nope_val, F32), rope,
                            jnp.zeros((t, LANES - MLA_NOPE - MLA_ROPE), F32)], axis=-1)


def kernel(x, c, ctx, c_ctx, w_mod, b_mod, g_pre_mix, g_post_mix, g_pre_ffn, g_post_ffn, w_in, g_q_lora, w_uq,
           g_kv_lora, w_ukv, w_o_mla, conv_qkv, a_log, dt_bias, g_dn_out, w_o_dn, w_out, w_ffn_in, w_ffn_out):
    B, T, D = x.shape
    S = ctx.shape[1]
    assert w_mod.shape[0] == 1 and D == D_MODEL
    assert T % max(ROW_TILE, DN_CHUNK, Q_TILE) == 0 and S % DN_CHUNK == 0 and T % GRID_W == 0
    L = 0
    row2 = lambda a: a[L].reshape(1, -1)

    mod_rows = -(-(B + 1) // 8) * 8
    cc = jnp.concatenate([c, c_ctx[None, :], jnp.zeros((mod_rows - B - 1, D), F32)], axis=0)
    mod3 = _modulation(cc, w_mod[L], b_mod[L]).reshape(mod_rows, 1, 6 * D)

    wi = w_in[L]
    o_cq, o_ckv, o_kr = 0, Q_LORA, Q_LORA + KV_LORA
    o_qkv = o_kr + MLA_ROPE
    o_z = o_qkv + 2 * DN_QK + DN_VW
    o_beta = o_z + DN_VW
    o_gate = o_beta + 4 * DN_HEADS
    w_kr = wi[:, o_kr:o_qkv]
    zc = lambda n: jnp.zeros((D, n), F32)
    w_pack = jnp.concatenate([
        wi[:, o_cq:o_kr],
        wi[:, o_beta:o_gate], zc(MLA_NOPE - 4 * DN_HEADS), w_kr, zc(LANES - MLA_NOPE - MLA_ROPE),
        zc(MLA_NOPE), _rot_half(w_kr), zc(LANES - MLA_NOPE - MLA_ROPE),
        wi[:, o_qkv:o_beta], wi[:, o_gate:]], axis=1).astype(BF16)
    assert w_pack.shape[1] == C_END

    dq = MLA_NOPE + MLA_ROPE
    wq = w_uq[L].reshape(Q_LORA, MLA_HEADS, dq)
    zq = jnp.zeros((Q_LORA, MLA_HEADS, LANES - dq), F32)
    wq1 = jnp.concatenate([wq, zq], axis=-1).reshape(Q_LORA, MLA_HEADS * LANES).astype(BF16)
    wq2 = jnp.concatenate([jnp.zeros((Q_LORA, MLA_HEADS, MLA_NOPE), F32), _rot_half(wq[..., MLA_NOPE:]), zq],
                          axis=-1).reshape(Q_LORA, MLA_HEADS * LANES).astype(BF16)
    wkv = w_ukv[L].reshape(KV_LORA, MLA_HEADS, MLA_NOPE + MLA_V)
    wk = jnp.concatenate([wkv[..., :MLA_NOPE], jnp.zeros((KV_LORA, MLA_HEADS, LANES - MLA_NOPE), F32)],
                         axis=-1).reshape(KV_LORA, MLA_HEADS * LANES).astype(BF16)
    wv = wkv[..., MLA_NOPE:].reshape(KV_LORA, MLA_HEADS * MLA_V).astype(BF16)

    cos, sin = _rope_tables(T // GRID_W)
    qs = dq ** -0.5 * LOG2E
    tabs_lat = (_pad_tab(qs, cos * qs, T), _pad_tab(0.0, sin * qs, T), _pad_tab(0.0, cos, T), _pad_tab(0.0, sin, T))
    ones = jnp.ones((S, MLA_ROPE), F32)
    no_rot = _pad_tab(0.0, 0.0 * ones, S)
    tabs_ctx = (no_rot, no_rot, _pad_tab(0.0, ones, S), no_rot)

    x2 = x.reshape(B * T, D)
    c2 = ctx.reshape(B * S, D)
    gpre, gq, gkv = row2(g_pre_mix), row2(g_q_lora), row2(g_kv_lora)
    q, k, v, qkv, ba, z, gates = _input_projection(
        x2, mod3, T, T, w_pack, gpre, gq, wq1, wq2, gkv, wk, wv, tabs_lat, True)
    k_c, v_c, qkv_c, ba_c = _input_projection(
        c2, mod3[B:B + 1], B * S, S, w_pack[:, :C_CTX_END], gpre, gq, wq1, wq2, gkv, wk, wv, tabs_ctx, False)

    r3 = lambda a, n: a.reshape(B, n, a.shape[-1])
    o_mla = _attention(r3(q, T), r3(k, T), r3(k_c, S), r3(v, T), r3(v_c, S))

    lane_pad = lambda a, fill: jnp.pad(a.astype(F32).reshape(1, -1), ((0, 0), (8, LANES - 16)), constant_values=fill)
    alog = lane_pad(a_log[L], -1e30)
    dtb = lane_pad(dt_bias[L], 0.0)
    cw = conv_qkv[L].astype(F32)
    ctx_ops = _dn_prepare(r3(qkv_c, S), r3(ba_c, S), cw, alog, dtb, False)
    lat_ops = _dn_prepare(r3(qkv, T), r3(ba, T), cw, alog, dtb, True)
    o_dn2 = _dn_scan(ctx_ops, lat_ops)

    out = _merge_ffn(x2, T, o_mla.reshape(B * T, -1), o_dn2, z, gates, mod3, row2(g_dn_out),
                     w_o_mla[L].astype(BF16), w_o_dn[L].astype(BF16), w_out[L].astype(BF16), row2(g_post_mix),
                     row2(g_pre_ffn), w_ffn_in[L].astype(BF16), w_ffn_out[L].astype(BF16), row2(g_post_ffn))
    return out.reshape(B, T, D)
```

```python
import functools
import math

import jax
import jax.numpy as jnp
from jax import lax
from jax.experimental import pallas as pl
from jax.experimental.pallas import tpu as pltpu

F32 = jnp.float32
BF16 = jnp.bfloat16

D_MODEL = 1024
GRID_W = 64
MLA_HEADS = 8
MLA_NOPE = 64
MLA_ROPE = 32
MLA_V = 64
Q_LORA = 512
KV_LORA = 256
ROPE_BASE = 10000.0
DN_HEADS = 4
DN_DK = 128
DN_DV = 128
DN_QK = DN_HEADS * DN_DK
DN_VW = DN_HEADS * DN_DV
FF_HIDDEN = 2816
EPS = 1e-6

LANES = 128
DN_CHUNK = 128
ROW_TILE = 512
Q_TILE = 256
FF_TILE = 256
PACK_ROWS = 16
VMEM_LIMIT = 56 * 1024 * 1024
LOG2E = math.log2(math.e)

C_CQ = 0
C_CKV = C_CQ + Q_LORA
C_M1 = C_CKV + KV_LORA
C_M2 = C_M1 + LANES
C_QKV = C_M2 + LANES
C_Z = C_QKV + 2 * DN_QK + DN_VW
C_GATE = C_Z + DN_VW
C_END = C_GATE + 2 * D_MODEL
C_CTX_END = C_Z


def _mm(a, b):
    return jnp.dot(a, b, preferred_element_type=F32)


def _mm_nt(a, b):
    return lax.dot_general(a, b, (((1,), (1,)), ((), ())), preferred_element_type=F32)


def _sigmoid(x):
    return 1.0 / (1.0 + jnp.exp(-x))


def _silu(x):
    return x * _sigmoid(x)


def _rms(x, g):
    return x * lax.rsqrt(jnp.mean(x * x, axis=-1, keepdims=True) + EPS) * g


def _params(n_axes):
    return pltpu.CompilerParams(dimension_semantics=("arbitrary",) * n_axes, vmem_limit_bytes=VMEM_LIMIT)


def _const_spec(shape):
    nd = len(shape)
    return pl.BlockSpec(shape, lambda *_: (0,) * nd, pipeline_mode=pl.Buffered(1))


def _mod_body(c_ref, w_ref, b_ref, o_ref):
    s = _silu(c_ref[...])
    o_ref[...] = jnp.dot(s, w_ref[...], preferred_element_type=F32, precision=lax.Precision.HIGHEST) + b_ref[...]


def _modulation(cc, w_mod, b_mod):
    rows, d = cc.shape
    n = w_mod.shape[1]
    tn = 1024
    return pl.pallas_call(
        _mod_body,
        grid=(n // tn,),
        in_specs=[pl.BlockSpec((rows, d), lambda j: (0, 0)),
                  pl.BlockSpec((d, tn), lambda j: (0, j)),
                  pl.BlockSpec((1, tn), lambda j: (0, j))],
        out_specs=pl.BlockSpec((rows, tn), lambda j: (0, j)),
        out_shape=jax.ShapeDtypeStruct((rows, n), F32),
        compiler_params=_params(1),
        name="modulation",
    )(cc, w_mod, b_mod.reshape(1, n))


def _inproj_body(x_ref, sh_ref, sc_ref, gpre_ref, w_ref, gq_ref, wq1_ref, wq2_ref, gkv_ref, wk_ref, wv_ref,
                 cq_tab, sq_tab, ck_tab, sk_tab, *outs, full):
    if full:
        q_ref, k_ref, v_ref, qkv_ref, ba_ref, z_ref, gate_ref = outs
    else:
        k_ref, v_ref, qkv_ref, ba_ref = outs
    u = (_rms(x_ref[...], gpre_ref[...]) * (1.0 + sc_ref[...]) + sh_ref[...]).astype(BF16)

    if full:
        nq = _rms(_mm(u, w_ref[:, C_CQ:C_CKV]), gq_ref[...]).astype(BF16)
        qa = _mm(nq, wq1_ref[...])
        qb = _mm(nq, wq2_ref[...])
        cq = cq_tab[...]
        sq = sq_tab[...]
        for h in range(MLA_HEADS):
            sl = slice(h * LANES, (h + 1) * LANES)
            q_ref[:, sl] = (qa[:, sl] * cq + qb[:, sl] * sq).astype(BF16)

    nkv = _rms(_mm(u, w_ref[:, C_CKV:C_M1]), gkv_ref[...]).astype(BF16)
    m1 = _mm(u, w_ref[:, C_M1:C_M2])
    m2 = _mm(u, w_ref[:, C_M2:C_QKV])
    ba_ref[...] = m1
    k_rope = m1 * ck_tab[...] + m2 * sk_tab[...]
    kn = _mm(nkv, wk_ref[...])
    for h in range(MLA_HEADS):
        sl = slice(h * LANES, (h + 1) * LANES)
        k_ref[:, sl] = (kn[:, sl] + k_rope).astype(BF16)
    v_ref[...] = _mm(nkv, wv_ref[...]).astype(BF16)
    qkv_ref[...] = _mm(u, w_ref[:, C_QKV:C_Z]).astype(BF16)
    if full:
        z_ref[...] = _mm(u, w_ref[:, C_Z:C_GATE]).astype(BF16)
        gate_ref[...] = _sigmoid(_mm(u, w_ref[:, C_GATE:C_END])).astype(BF16)


def _input_projection(x2, mod3, mod_period, tab_period, w, gpre, gq, wq1, wq2, gkv, wk, wv, tabs, full):
    n, d = x2.shape
    tm = min(ROW_TILE, tab_period)
    mod_tiles = mod_period // tm
    tab_tiles = tab_period // tm
    row = lambda i: (i, 0)
    tab = lambda i: (i % tab_tiles, 0)
    in_specs = [
        pl.BlockSpec((tm, d), row),
        pl.BlockSpec((None, 1, d), lambda i: (i // mod_tiles, 0, 0)),
        pl.BlockSpec((None, 1, d), lambda i: (i // mod_tiles, 0, 1)),
        _const_spec(gpre.shape), _const_spec(w.shape), _const_spec(gq.shape), _const_spec(wq1.shape),
        _const_spec(wq2.shape), _const_spec(gkv.shape), _const_spec(wk.shape), _const_spec(wv.shape),
        pl.BlockSpec((tm, LANES), tab), pl.BlockSpec((tm, LANES), tab), pl.BlockSpec((tm, LANES), tab),
        pl.BlockSpec((tm, LANES), tab),
    ]
    kw = MLA_HEADS * LANES
    vw = MLA_HEADS * MLA_V
    qkvw = 2 * DN_QK + DN_VW
    outs = [(kw, BF16), (vw, BF16), (qkvw, BF16), (LANES, F32)]
    if full:
        outs = [(kw, BF16)] + outs + [(DN_VW, BF16), (2 * D_MODEL, BF16)]
    return pl.pallas_call(
        functools.partial(_inproj_body, full=full),
        grid=(n // tm,),
        in_specs=in_specs,
        out_specs=[pl.BlockSpec((tm, c), row) for c, _ in outs],
        out_shape=[jax.ShapeDtypeStruct((n, c), t) for c, t in outs],
        compiler_params=_params(1),
        name="input_projection" if full else "input_projection_ctx",
    )(x2, mod3, mod3, gpre, w, gq, wq1, wq2, gkv, wk, wv, *tabs)


def _attn_body(q_ref, kl_ref, kc_ref, vl_ref, vc_ref, o_ref):
    for h in range(MLA_HEADS):
        sl = slice(h * LANES, (h + 1) * LANES)
        sv = slice(h * MLA_V, (h + 1) * MLA_V)
        qh = q_ref[:, sl]
        s1 = _mm_nt(qh, kl_ref[:, sl])
        s2 = _mm_nt(qh, kc_ref[:, sl])
        m = jnp.maximum(jnp.max(s1, axis=-1, keepdims=True), jnp.max(s2, axis=-1, keepdims=True))
        p1 = jnp.exp2(s1 - m)
        p2 = jnp.exp2(s2 - m)
        l = jnp.sum(p1, axis=-1, keepdims=True) + jnp.sum(p2, axis=-1, keepdims=True)
        o = _mm(p1.astype(BF16), vl_ref[:, sv]) + _mm(p2.astype(BF16), vc_ref[:, sv])
        o_ref[:, sv] = (o / l).astype(BF16)


def _attention(q, kl, kc, vl, vc):
    b, t, kw = q.shape
    s = kc.shape[1]
    vw = vl.shape[2]
    tq = min(Q_TILE, t)
    return pl.pallas_call(
        _attn_body,
        grid=(b, t // tq),
        in_specs=[pl.BlockSpec((None, tq, kw), lambda i, j: (i, j, 0)),
                  pl.BlockSpec((None, t, kw), lambda i, j: (i, 0, 0)),
                  pl.BlockSpec((None, s, kw), lambda i, j: (i, 0, 0)),
                  pl.BlockSpec((None, t, vw), lambda i, j: (i, 0, 0)),
                  pl.BlockSpec((None, s, vw), lambda i, j: (i, 0, 0))],
        out_specs=pl.BlockSpec((None, tq, vw), lambda i, j: (i, j, 0)),
        out_shape=jax.ShapeDtypeStruct((b, t, vw), BF16),
        compiler_params=_params(2),
        name="attention",
    )(q, kl, kc, vl, vc)


def _dn_prep_body(x_ref, xp_ref, xn_ref, ba_ref, cw_ref, alog_ref, dtb_ref, *outs, nchunks, emit):
    if emit:
        u_ref, w_ref, kdt_ref, dl_ref, qg_ref, qkm_ref = outs
    else:
        u_ref, w_ref, kdt_ref, dl_ref = outs
    c = pl.program_id(1)
    C = DN_CHUNK
    has_prev = jnp.where(c > 0, 1.0, 0.0)
    has_next = jnp.where(c < nchunks - 1, 1.0, 0.0)
    rowi = lax.broadcasted_iota(jnp.int32, (C, C), 0)
    coli = lax.broadcasted_iota(jnp.int32, (C, C), 1)

    def conv_slab(s):
        sl = slice(s * LANES, (s + 1) * LANES)
        x = x_ref[:, sl].astype(F32)
        prev = xp_ref[PACK_ROWS - 1:PACK_ROWS, sl].astype(F32) * has_prev
        nxt = xn_ref[0:1, sl].astype(F32) * has_next
        xd = jnp.where(rowi == 0, prev, pltpu.roll(x, 1, 0))
        xu = jnp.where(rowi == C - 1, nxt, pltpu.roll(x, C - 1, 0))
        return _silu(xd * cw_ref[0:1, sl] + x * cw_ref[1:2, sl] + xu * cw_ref[2:3, sl])

    def l2n(a):
        return a * lax.rsqrt(jnp.sum(a * a, axis=-1, keepdims=True) + EPS)

    bg = ba_ref[...]
    beta = _sigmoid(bg)
    xs = bg + dtb_ref[...]
    softplus = jnp.maximum(xs, 0.0) + jnp.log(1.0 + jnp.exp(-jnp.abs(xs)))
    g = -jnp.exp(alog_ref[...]) * softplus
    gc = g
    sft = 1
    while sft < C:
        gc = gc + jnp.where(rowi >= sft, pltpu.roll(gc, sft, 0), 0.0)
        sft *= 2
    gtot = gc[C - 1:C, :]
    gcr = gtot - gc + g
    gc_t = gc.T
    gcr_t = gcr.T
    e_gc = jnp.exp(gc)
    e_gcr = jnp.exp(gcr)
    e_rem = jnp.exp(gtot - gc)
    e_rem_r = jnp.exp(gtot - gcr)
    dl_all = jnp.exp(jnp.broadcast_to(gc_t[8:16, C - 1:C], (8, LANES)))
    scale = DN_DK ** -0.5

    systems = []
    for h in range(DN_HEADS):
        hs = slice(h * LANES, (h + 1) * LANES)
        qh = l2n(conv_slab(h)) * scale
        kh = l2n(conv_slab(DN_HEADS + h))
        vh = conv_slab(2 * DN_HEADS + h)
        khb = kh.astype(BF16)
        kk = _mm_nt(khb, khb)
        if emit:
            qk = _mm_nt(qh.astype(BF16), khb)
        for d in range(2):
            lane = 8 + 4 * d + h
            gsrc, gsrc_t, esrc, erem = (gc, gc_t, e_gc, e_rem) if d == 0 else (gcr, gcr_t, e_gcr, e_rem_r)
            gcol = gsrc[:, lane:lane + 1]
            grow = gsrc_t[lane:lane + 1, :]
            bcol = beta[:, 4 * d + h:4 * d + h + 1]
            incl = (rowi >= coli) if d == 0 else (rowi <= coli)
            strict = (rowi > coli) if d == 0 else (rowi < coli)
            decay = jnp.exp(jnp.where(incl, gcol - grow, -1e30))
            ecol = esrc[:, lane:lane + 1]
            systems.append((d, hs, jnp.where(strict, -(kk * bcol) * decay, 0.0),
                            jnp.concatenate([vh * bcol, kh * (bcol * ecol)], axis=1)))
            kdt_ref[d, :, hs] = (kh * erem[:, lane:lane + 1]).T.astype(BF16)
            dl_ref[d, h:h + 1, :] = dl_all[4 * d + h:4 * d + h + 1, :]
            if emit:
                qg_ref[d, :, hs] = (qh * ecol).astype(BF16)
                qkm_ref[d, :, hs] = jnp.where(incl, qk * decay, 0.0).astype(BF16)

    rs = [jnp.where((rowi >> 1) == (coli >> 1), nmat, 0.0) for _, _, nmat, _ in systems]
    for lg in range(1, int(math.log2(C))):
        cross = ((rowi >> (lg + 1)) == (coli >> (lg + 1))) & ((rowi >> lg) != (coli >> lg))
        rbs = [r.astype(BF16) for r in rs]
        ms = []
        for (_, _, nmat, _), rb in zip(systems, rbs):
            ns = jnp.where(cross, nmat, 0.0)
            ms.append(ns + _mm(rb, ns.astype(BF16)))
        rs = [r + m + _mm(m.astype(BF16), rb) for r, m, rb in zip(rs, ms, rbs)]
    for (d, hs, _, rhs), r in zip(systems, rs):
        uw = rhs + _mm(r.astype(BF16), rhs.astype(BF16))
        u_ref[d, :, hs] = uw[:, :LANES].astype(BF16)
        w_ref[d, :, hs] = uw[:, LANES:].astype(BF16)


def _dn_prepare(qkv, ba, conv_w, alog, dtb, emit):
    b, t, cw = qkv.shape
    C = DN_CHUNK
    nchunks = t // C
    per = C // PACK_ROWS
    last = t // PACK_ROWS - 1
    seq_out = pl.BlockSpec((None, 2, C, DN_VW), lambda i, c: (i, 0, c, 0))
    seq_shape = jax.ShapeDtypeStruct((b, 2, t, DN_VW), BF16)
    out_specs = [seq_out, seq_out, seq_out, pl.BlockSpec((None, 2, None, DN_HEADS, LANES), lambda i, c: (i, 0, c, 0, 0))]
    out_shape = [seq_shape, seq_shape, seq_shape, jax.ShapeDtypeStruct((b, 2, nchunks, DN_HEADS, LANES), F32)]
    if emit:
        out_specs += [seq_out, seq_out]
        out_shape += [seq_shape, seq_shape]
    return pl.pallas_call(
        functools.partial(_dn_prep_body, nchunks=nchunks, emit=emit),
        grid=(b, nchunks),
        in_specs=[pl.BlockSpec((None, C, cw), lambda i, c: (i, c, 0)),
                  pl.BlockSpec((None, PACK_ROWS, cw), lambda i, c: (i, jnp.maximum(c * per - 1, 0), 0)),
                  pl.BlockSpec((None, PACK_ROWS, cw), lambda i, c: (i, jnp.minimum((c + 1) * per, last), 0)),
                  pl.BlockSpec((None, C, LANES), lambda i, c: (i, c, 0)),
                  _const_spec(conv_w.shape), _const_spec(alog.shape), _const_spec(dtb.shape)],
        out_specs=out_specs,
        out_shape=out_shape,
        compiler_params=_params(2),
        name="deltanet_prepare" if emit else "deltanet_prepare_ctx",
    )(qkv, qkv, qkv, ba, conv_w, alog, dtb)


def _dn_scan_body(uc_ref, wc_ref, kdc_ref, dlc_ref, ul_ref, wl_ref, kdl_ref, dll_ref, qg_ref, qkm_ref, o_ref, s_ref,
                  *, n_ctx, n_lat):
    d = pl.program_id(1)
    C = DN_CHUNK
    s_ref[...] = jnp.zeros_like(s_ref)

    def step(c, n, u_ref, w_ref, kdt_ref, dl_ref, emit):
        ce = jnp.where(d == 0, c, n - 1 - c)
        rows = pl.ds(pl.multiple_of(ce * C, C), C)
        heads = range(DN_HEADS)
        hs = [slice(h * LANES, (h + 1) * LANES) for h in heads]
        ss = [s_ref[h] for h in heads]
        sbs = [s.astype(BF16) for s in ss]
        ws = [_mm(w_ref[rows, hs[h]], sbs[h]) for h in heads]
        if emit:
            os = [_mm(qg_ref[rows, hs[h]], sbs[h]) for h in heads]
        vnbs = [(u_ref[rows, hs[h]].astype(F32) - ws[h]).astype(BF16) for h in heads]
        for h in heads:
            s_ref[h] = ss[h] * dl_ref[ce, h:h + 1, :] + _mm(kdt_ref[rows, hs[h]], vnbs[h])
        if emit:
            for h in heads:
                o_ref[rows, hs[h]] = (os[h] + _mm(qkm_ref[rows, hs[h]], vnbs[h])).astype(BF16)

    for c in range(n_ctx):
        step(c, n_ctx, uc_ref, wc_ref, kdc_ref, dlc_ref, False)

    def body(c, carry):
        step(c, n_lat, ul_ref, wl_ref, kdl_ref, dll_ref, True)
        return carry

    lax.fori_loop(0, n_lat, body, 0)


def _dn_scan(ctx_ops, lat_ops):
    uc, wc, kdc, dlc = ctx_ops
    ul, wl, kdl, dll, qg, qkm = lat_ops
    b, _, t, vw = ul.shape
    s = uc.shape[2]
    n_ctx, n_lat = s // DN_CHUNK, t // DN_CHUNK
    seq = lambda n: pl.BlockSpec((None, None, n, vw), lambda i, d: (i, d, 0, 0))
    dls = lambda n: pl.BlockSpec((None, None, n, DN_HEADS, LANES), lambda i, d: (i, d, 0, 0, 0))
    return pl.pallas_call(
        functools.partial(_dn_scan_body, n_ctx=n_ctx, n_lat=n_lat),
        grid=(b, 2),
        in_specs=[seq(s), seq(s), seq(s), dls(n_ctx), seq(t), seq(t), seq(t), dls(n_lat), seq(t), seq(t)],
        out_specs=seq(t),
        out_shape=jax.ShapeDtypeStruct((b, 2, t, vw), BF16),
        scratch_shapes=[pltpu.VMEM((DN_HEADS, DN_DK, DN_DV), F32)],
        compiler_params=_params(2),
        name="deltanet_scan",
    )(uc, wc, kdc, dlc, ul, wl, kdl, dll, qg, qkm)


def _merge_ffn_body(x_ref, om_ref, of_ref, ob_ref, z_ref, gate_ref, gt1_ref, sh2_ref, sc2_ref, gt2_ref,
                    gdn_ref, woa_ref, wod_ref, wout_ref, gpost_ref, gpre2_ref, w1_ref, w2_ref, gpost2_ref, out_ref):
    odn = of_ref[...].astype(F32) + ob_ref[...].astype(F32)
    parts = []
    for h in range(DN_HEADS):
        hs = slice(h * DN_DV, (h + 1) * DN_DV)
        parts.append(_rms(odn[:, hs], gdn_ref[...]) * _silu(z_ref[:, hs].astype(F32)))
    o_dn = jnp.concatenate(parts, axis=1).astype(BF16)
    ya = _mm(om_ref[...], woa_ref[...])
    yb = _mm(o_dn, wod_ref[...])
    mix = gate_ref[:, :D_MODEL].astype(F32) * ya + gate_ref[:, D_MODEL:].astype(F32) * yb
    y = _mm(mix.astype(BF16), wout_ref[...])
    x1 = x_ref[...] + gt1_ref[...] * _rms(y, gpost_ref[...])
    u2 = (_rms(x1, gpre2_ref[...]) * (1.0 + sc2_ref[...]) + sh2_ref[...]).astype(BF16)
    acc = jnp.zeros(x1.shape, F32)
    for j in range(FF_HIDDEN // FF_TILE):
        a = _mm(u2, w1_ref[:, j * FF_TILE:(j + 1) * FF_TILE])
        up = _mm(u2, w1_ref[:, FF_HIDDEN + j * FF_TILE:FF_HIDDEN + (j + 1) * FF_TILE])
        acc = acc + _mm((_silu(a) * up).astype(BF16), w2_ref[j * FF_TILE:(j + 1) * FF_TILE, :])
    out_ref[...] = x1 + gt2_ref[...] * _rms(acc, gpost2_ref[...])


def _merge_ffn(x2, seq, om, o_dn2, z, gates, mod3, gdn, woa, wod, wout, gpost, gpre2, w1, w2, gpost2):
    n, d = x2.shape
    tm = min(ROW_TILE, seq)
    tiles = seq // tm
    nb = n // seq
    row = lambda i: (i, 0)
    modspec = lambda k: pl.BlockSpec((None, 1, d), lambda i: (i // tiles, 0, k))
    dn_spec = lambda dirn: pl.BlockSpec((None, None, tm, DN_VW), lambda i: (i // tiles, dirn, i % tiles, 0))
    return pl.pallas_call(
        _merge_ffn_body,
        grid=(n // tm,),
        in_specs=[pl.BlockSpec((tm, d), row), pl.BlockSpec((tm, om.shape[1]), row), dn_spec(0), dn_spec(1),
                  pl.BlockSpec((tm, DN_VW), row), pl.BlockSpec((tm, 2 * d), row),
                  modspec(2), modspec(3), modspec(4), modspec(5),
                  _const_spec(gdn.shape), _const_spec(woa.shape), _const_spec(wod.shape), _const_spec(wout.shape),
                  _const_spec(gpost.shape), _const_spec(gpre2.shape), _const_spec(w1.shape), _const_spec(w2.shape),
                  _const_spec(gpost2.shape)],
        out_specs=pl.BlockSpec((tm, d), row),
        out_shape=jax.ShapeDtypeStruct((n, d), F32),
        compiler_params=_params(1),
        name="merge_ffn",
    )(x2, om, o_dn2, o_dn2, z, gates, mod3, mod3, mod3, mod3, gdn, woa, wod, wout, gpost, gpre2, w1, w2, gpost2)


def _rot_half(w):
    n = MLA_ROPE // 4
    return jnp.concatenate([-w[..., n:2 * n], w[..., 0:n], -w[..., 3 * n:4 * n], w[..., 2 * n:3 * n]], axis=-1)


def _rope_tables(rows):
    row = jnp.repeat(jnp.arange(rows, dtype=F32), GRID_W)
    col = jnp.tile(jnp.arange(GRID_W, dtype=F32), rows)
    n = MLA_ROPE // 4
    inv = ROPE_BASE ** (-jnp.arange(n, dtype=F32) / n)
    ar, ac = row[:, None] * inv, col[:, None] * inv
    cos = jnp.concatenate([jnp.cos(ar), jnp.cos(ar), jnp.cos(ac), jnp.cos(ac)], axis=-1)
    sin = jnp.concatenate([jnp.sin(ar), jnp.sin(ar), jnp.sin(ac), jnp.sin(ac)], axis=-1)
    return cos, sin


def _pad_tab(nope_val, rope, t):
    return jnp.concatenate([jnp.full((t, MLA_NOPE), nope_val, F32), rope,
                            jnp.zeros((t, LANES - MLA_NOPE - MLA_ROPE), F32)], axis=-1)


def kernel(x, c, ctx, c_ctx, w_mod, b_mod, g_pre_mix, g_post_mix, g_pre_ffn, g_post_ffn, w_in, g_q_lora, w_uq,
           g_kv_lora, w_ukv, w_o_mla, conv_qkv, a_log, dt_bias, g_dn_out, w_o_dn, w_out, w_ffn_in, w_ffn_out):
    B, T, D = x.shape
    S = ctx.shape[1]
    assert w_mod.shape[0] == 1 and D == D_MODEL
    assert T % max(ROW_TILE, DN_CHUNK, Q_TILE) == 0 and S % DN_CHUNK == 0 and T % GRID_W == 0
    L = 0
    row2 = lambda a: a[L].reshape(1, -1)

    mod_rows = -(-(B + 1) // 8) * 8
    cc = jnp.concatenate([c, c_ctx[None, :], jnp.zeros((mod_rows - B - 1, D), F32)], axis=0)
    mod3 = _modulation(cc, w_mod[L], b_mod[L]).reshape(mod_rows, 1, 6 * D)

    wi = w_in[L]
    o_cq, o_ckv, o_kr = 0, Q_LORA, Q_LORA + KV_LORA
    o_qkv = o_kr + MLA_ROPE
    o_z = o_qkv + 2 * DN_QK + DN_VW
    o_beta = o_z + DN_VW
    o_gate = o_beta + 4 * DN_HEADS
    w_kr = wi[:, o_kr:o_qkv]
    zc = lambda n: jnp.zeros((D, n), F32)
    w_pack = jnp.concatenate([
        wi[:, o_cq:o_kr],
        wi[:, o_beta:o_gate], zc(MLA_NOPE - 4 * DN_HEADS), w_kr, zc(LANES - MLA_NOPE - MLA_ROPE),
        zc(MLA_NOPE), _rot_half(w_kr), zc(LANES - MLA_NOPE - MLA_ROPE),
        wi[:, o_qkv:o_beta], wi[:, o_gate:]], axis=1).astype(BF16)
    assert w_pack.shape[1] == C_END

    dq = MLA_NOPE + MLA_ROPE
    wq = w_uq[L].reshape(Q_LORA, MLA_HEADS, dq)
    zq = jnp.zeros((Q_LORA, MLA_HEADS, LANES - dq), F32)
    wq1 = jnp.concatenate([wq, zq], axis=-1).reshape(Q_LORA, MLA_HEADS * LANES).astype(BF16)
    wq2 = jnp.concatenate([jnp.zeros((Q_LORA, MLA_HEADS, MLA_NOPE), F32), _rot_half(wq[..., MLA_NOPE:]), zq],
                          axis=-1).reshape(Q_LORA, MLA_HEADS * LANES).astype(BF16)
    wkv = w_ukv[L].reshape(KV_LORA, MLA_HEADS, MLA_NOPE + MLA_V)
    wk = jnp.concatenate([wkv[..., :MLA_NOPE], jnp.zeros((KV_LORA, MLA_HEADS, LANES - MLA_NOPE), F32)],
                         axis=-1).reshape(KV_LORA, MLA_HEADS * LANES).astype(BF16)
    wv = wkv[..., MLA_NOPE:].reshape(KV_LORA, MLA_HEADS * MLA_V).astype(BF16)

    cos, sin = _rope_tables(T // GRID_W)
    qs = dq ** -0.5 * LOG2E
    tabs_lat = (_pad_tab(qs, cos * qs, T), _pad_tab(0.0, sin * qs, T), _pad_tab(0.0, cos, T), _pad_tab(0.0, sin, T))
    ones = jnp.ones((S, MLA_ROPE), F32)
    no_rot = _pad_tab(0.0, 0.0 * ones, S)
    tabs_ctx = (no_rot, no_rot, _pad_tab(0.0, ones, S), no_rot)

    x2 = x.reshape(B * T, D)
    c2 = ctx.reshape(B * S, D)
    gpre, gq, gkv = row2(g_pre_mix), row2(g_q_lora), row2(g_kv_lora)
    q, k, v, qkv, ba, z, gates = _input_projection(
        x2, mod3, T, T, w_pack, gpre, gq, wq1, wq2, gkv, wk, wv, tabs_lat, True)
    k_c, v_c, qkv_c, ba_c = _input_projection(
        c2, mod3[B:B + 1], B * S, S, w_pack[:, :C_CTX_END], gpre, gq, wq1, wq2, gkv, wk, wv, tabs_ctx, False)

    r3 = lambda a, n: a.reshape(B, n, a.shape[-1])
    o_mla = _attention(r3(q, T), r3(k, T), r3(k_c, S), r3(v, T), r3(v_c, S))

    lane_pad = lambda a, fill: jnp.pad(a.astype(F32).reshape(1, -1), ((0, 0), (8, LANES - 16)), constant_values=fill)
    alog = lane_pad(a_log[L], -1e30)
    dtb = lane_pad(dt_bias[L], 0.0)
    cw = conv_qkv[L].astype(F32)
    ctx_ops = _dn_prepare(r3(qkv_c, S), r3(ba_c, S), cw, alog, dtb, False)
    lat_ops = _dn_prepare(r3(qkv, T), r3(ba, T), cw, alog, dtb, True)
    o_dn2 = _dn_scan(ctx_ops, lat_ops)

    out = _merge_ffn(x2, T, o_mla.reshape(B * T, -1), o_dn2, z, gates, mod3, row2(g_dn_out),
                     w_o_mla[L].astype(BF16), w_o_dn[L].astype(BF16), w_out[L].astype(BF16), row2(g_post_mix),
                     row2(g_pre_ffn), w_ffn_in[L].astype(BF16), w_ffn_out[L].astype(BF16), row2(g_post_ffn))
    return out.reshape(B, T, D)
```

```python
import functools
import math

import jax
import jax.numpy as jnp
from jax import lax
from jax.experimental import pallas as pl
from jax.experimental.pallas import tpu as pltpu

F32 = jnp.float32
BF16 = jnp.bfloat16

D_MODEL = 1024
GRID_W = 64
MLA_HEADS = 8
MLA_NOPE = 64
MLA_ROPE = 32
MLA_V = 64
Q_LORA = 512
KV_LORA = 256
ROPE_BASE = 10000.0
DN_HEADS = 4
DN_DK = 128
DN_DV = 128
DN_QK = DN_HEADS * DN_DK
DN_VW = DN_HEADS * DN_DV
FF_HIDDEN = 2816
EPS = 1e-6

LANES = 128
DN_CHUNK = 128
ROW_TILE = 512
Q_TILE = 512
V_GROUP = 4
FF_TILE = 256
PACK_ROWS = 16
VMEM_LIMIT = 56 * 1024 * 1024
LOG2E = math.log2(math.e)

C_CQ = 0
C_CKV = C_CQ + Q_LORA
C_M1 = C_CKV + KV_LORA
C_M2 = C_M1 + LANES
C_QKV = C_M2 + LANES
C_Z = C_QKV + 2 * DN_QK + DN_VW
C_GATE = C_Z + DN_VW
C_END = C_GATE + 2 * D_MODEL
C_CTX_END = C_Z


def _mm(a, b):
    return jnp.dot(a, b, preferred_element_type=F32)


def _mm_nt(a, b):
    return lax.dot_general(a, b, (((1,), (1,)), ((), ())), preferred_element_type=F32)


def _sigmoid(x):
    return 1.0 / (1.0 + jnp.exp(-x))


def _silu(x):
    return x * _sigmoid(x)


def _rms(x, g):
    return x * lax.rsqrt(jnp.mean(x * x, axis=-1, keepdims=True) + EPS) * g


def _params(n_axes):
    return pltpu.CompilerParams(dimension_semantics=("arbitrary",) * n_axes, vmem_limit_bytes=VMEM_LIMIT)


def _const_spec(shape):
    nd = len(shape)
    return pl.BlockSpec(shape, lambda *_: (0,) * nd, pipeline_mode=pl.Buffered(1))


def _mod_body(c_ref, w_ref, b_ref, o_ref):
    s = _silu(c_ref[...])
    o_ref[...] = jnp.dot(s, w_ref[...], preferred_element_type=F32, precision=lax.Precision.HIGHEST) + b_ref[...]


def _modulation(cc, w_mod, b_mod):
    rows, d = cc.shape
    n = w_mod.shape[1]
    tn = 1024
    return pl.pallas_call(
        _mod_body,
        grid=(n // tn,),
        in_specs=[pl.BlockSpec((rows, d), lambda j: (0, 0)),
                  pl.BlockSpec((d, tn), lambda j: (0, j)),
                  pl.BlockSpec((1, tn), lambda j: (0, j))],
        out_specs=pl.BlockSpec((rows, tn), lambda j: (0, j)),
        out_shape=jax.ShapeDtypeStruct((rows, n), F32),
        compiler_params=_params(1),
        name="modulation",
    )(cc, w_mod, b_mod.reshape(1, n))


def _inproj_body(x_ref, sh_ref, sc_ref, gpre_ref, w_ref, gq_ref, wq1_ref, wq2_ref, gkv_ref, wk_ref, wv_ref,
                 cq_tab, sq_tab, ck_tab, sk_tab, *outs, full):
    if full:
        q_ref, k_ref, v_ref, qkv_ref, ba_ref, z_ref, gate_ref = outs
    else:
        k_ref, v_ref, qkv_ref, ba_ref = outs
    u = (_rms(x_ref[...], gpre_ref[...]) * (1.0 + sc_ref[...]) + sh_ref[...]).astype(BF16)

    if full:
        nq = _rms(_mm(u, w_ref[:, C_CQ:C_CKV]), gq_ref[...]).astype(BF16)
        qa = _mm(nq, wq1_ref[...])
        qb = _mm(nq, wq2_ref[...])
        cq = cq_tab[...]
        sq = sq_tab[...]
        for h in range(MLA_HEADS):
            sl = slice(h * LANES, (h + 1) * LANES)
            q_ref[:, sl] = (qa[:, sl] * cq + qb[:, sl] * sq).astype(BF16)

    nkv = _rms(_mm(u, w_ref[:, C_CKV:C_M1]), gkv_ref[...]).astype(BF16)
    m1 = _mm(u, w_ref[:, C_M1:C_M2])
    m2 = _mm(u, w_ref[:, C_M2:C_QKV])
    ba_ref[...] = m1
    k_rope = m1 * ck_tab[...] + m2 * sk_tab[...]
    kn = _mm(nkv, wk_ref[...])
    for h in range(MLA_HEADS):
        sl = slice(h * LANES, (h + 1) * LANES)
        k_ref[:, sl] = (kn[:, sl] + k_rope).astype(BF16)
    v_ref[...] = _mm(nkv, wv_ref[...]).astype(BF16)
    qkv_ref[...] = _mm(u, w_ref[:, C_QKV:C_Z]).astype(BF16)
    if full:
        z_ref[...] = _mm(u, w_ref[:, C_Z:C_GATE]).astype(BF16)
        gate_ref[...] = _sigmoid(_mm(u, w_ref[:, C_GATE:C_END])).astype(BF16)


def _input_projection(x2, mod3, mod_period, tab_period, w, gpre, gq, wq1, wq2, gkv, wk, wv, tabs, full):
    n, d = x2.shape
    tm = min(ROW_TILE, tab_period)
    mod_tiles = mod_period // tm
    tab_tiles = tab_period // tm
    row = lambda i: (i, 0)
    tab = lambda i: (i % tab_tiles, 0)
    in_specs = [
        pl.BlockSpec((tm, d), row),
        pl.BlockSpec((None, 1, d), lambda i: (i // mod_tiles, 0, 0)),
        pl.BlockSpec((None, 1, d), lambda i: (i // mod_tiles, 0, 1)),
        _const_spec(gpre.shape), _const_spec(w.shape), _const_spec(gq.shape), _const_spec(wq1.shape),
        _const_spec(wq2.shape), _const_spec(gkv.shape), _const_spec(wk.shape), _const_spec(wv.shape),
        pl.BlockSpec((tm, LANES), tab), pl.BlockSpec((tm, LANES), tab), pl.BlockSpec((tm, LANES), tab),
        pl.BlockSpec((tm, LANES), tab),
    ]
    kw = MLA_HEADS * LANES
    vw = MLA_HEADS * MLA_V
    qkvw = 2 * DN_QK + DN_VW
    outs = [(kw, BF16), (vw, BF16), (qkvw, BF16), (LANES, F32)]
    if full:
        outs = [(kw, BF16)] + outs + [(DN_VW, BF16), (2 * D_MODEL, BF16)]
    return pl.pallas_call(
        functools.partial(_inproj_body, full=full),
        grid=(n // tm,),
        in_specs=in_specs,
        out_specs=[pl.BlockSpec((tm, c), row) for c, _ in outs],
        out_shape=[jax.ShapeDtypeStruct((n, c), t) for c, t in outs],
        compiler_params=_params(1),
        name="input_projection" if full else "input_projection_ctx",
    )(x2, mod3, mod3, gpre, w, gq, wq1, wq2, gkv, wk, wv, *tabs)


def _attn_body(q_ref, kl_ref, kc_ref, vl_ref, vc_ref, o_ref):
    for h in range(MLA_HEADS):
        sl = slice(h * LANES, (h + 1) * LANES)
        sv = slice(h * MLA_V, (h + 1) * MLA_V)
        qh = q_ref[:, sl]
        s1 = _mm_nt(qh, kl_ref[:, sl])
        s2 = _mm_nt(qh, kc_ref[:, sl])
        m = jnp.maximum(jnp.max(s1, axis=-1, keepdims=True), jnp.max(s2, axis=-1, keepdims=True))
        p1 = jnp.exp2(s1 - m)
        p2 = jnp.exp2(s2 - m)
        l = jnp.sum(p1, axis=-1, keepdims=True) + jnp.sum(p2, axis=-1, keepdims=True)
        g0 = h // V_GROUP * V_GROUP * MLA_V
        sg = slice(g0, g0 + V_GROUP * MLA_V)
        og = _mm(p1.astype(BF16), vl_ref[:, sg]) + _mm(p2.astype(BF16), vc_ref[:, sg])
        o = og[:, h % V_GROUP * MLA_V:(h % V_GROUP + 1) * MLA_V]
        o_ref[:, sv] = (o / l).astype(BF16)


def _attention(q, kl, kc, vl, vc):
    b, t, kw = q.shape
    s = kc.shape[1]
    vw = vl.shape[2]
    tq = min(Q_TILE, t)
    return pl.pallas_call(
        _attn_body,
        grid=(b, t // tq),
        in_specs=[pl.BlockSpec((None, tq, kw), lambda i, j: (i, j, 0)),
                  pl.BlockSpec((None, t, kw), lambda i, j: (i, 0, 0)),
                  pl.BlockSpec((None, s, kw), lambda i, j: (i, 0, 0)),
                  pl.BlockSpec((None, t, vw), lambda i, j: (i, 0, 0)),
                  pl.BlockSpec((None, s, vw), lambda i, j: (i, 0, 0))],
        out_specs=pl.BlockSpec((None, tq, vw), lambda i, j: (i, j, 0)),
        out_shape=jax.ShapeDtypeStruct((b, t, vw), BF16),
        compiler_params=_params(2),
        name="attention",
    )(q, kl, kc, vl, vc)


def _dn_prep_body(x_ref, xp_ref, xn_ref, ba_ref, cw_ref, alog_ref, dtb_ref, *outs, nchunks, emit):
    if emit:
        u_ref, w_ref, kdt_ref, dl_ref, qg_ref, qkm_ref = outs
    else:
        u_ref, w_ref, kdt_ref, dl_ref = outs
    c = pl.program_id(1)
    C = DN_CHUNK
    has_prev = jnp.where(c > 0, 1.0, 0.0)
    has_next = jnp.where(c < nchunks - 1, 1.0, 0.0)
    rowi = lax.broadcasted_iota(jnp.int32, (C, C), 0)
    coli = lax.broadcasted_iota(jnp.int32, (C, C), 1)

    def conv_slab(s):
        sl = slice(s * LANES, (s + 1) * LANES)
        x = x_ref[:, sl].astype(F32)
        prev = xp_ref[PACK_ROWS - 1:PACK_ROWS, sl].astype(F32) * has_prev
        nxt = xn_ref[0:1, sl].astype(F32) * has_next
        xd = jnp.where(rowi == 0, prev, pltpu.roll(x, 1, 0))
        xu = jnp.where(rowi == C - 1, nxt, pltpu.roll(x, C - 1, 0))
        return _silu(xd * cw_ref[0:1, sl] + x * cw_ref[1:2, sl] + xu * cw_ref[2:3, sl])

    def l2n(a):
        return a * lax.rsqrt(jnp.sum(a * a, axis=-1, keepdims=True) + EPS)

    bg = ba_ref[...]
    beta = _sigmoid(bg)
    xs = bg + dtb_ref[...]
    softplus = jnp.maximum(xs, 0.0) + jnp.log(1.0 + jnp.exp(-jnp.abs(xs)))
    g = -jnp.exp(alog_ref[...]) * softplus
    gc = g
    sft = 1
    while sft < C:
        gc = gc + jnp.where(rowi >= sft, pltpu.roll(gc, sft, 0), 0.0)
        sft *= 2
    gtot = gc[C - 1:C, :]
    gcr = gtot - gc + g
    gc_t = gc.T
    gcr_t = gcr.T
    e_gc = jnp.exp(gc)
    e_gcr = jnp.exp(gcr)
    e_rem = jnp.exp(gtot - gc)
    e_rem_r = jnp.exp(gtot - gcr)
    dl_all = jnp.exp(jnp.broadcast_to(gc_t[8:16, C - 1:C], (8, LANES)))
    scale = DN_DK ** -0.5

    systems = []
    for h in range(DN_HEADS):
        hs = slice(h * LANES, (h + 1) * LANES)
        qh = l2n(conv_slab(h)) * scale
        kh = l2n(conv_slab(DN_HEADS + h))
        vh = conv_slab(2 * DN_HEADS + h)
        khb = kh.astype(BF16)
        kk = _mm_nt(khb, khb)
        if emit:
            qk = _mm_nt(qh.astype(BF16), khb)
        for d in range(2):
            lane = 8 + 4 * d + h
            gsrc, gsrc_t, esrc, erem = (gc, gc_t, e_gc, e_rem) if d == 0 else (gcr, gcr_t, e_gcr, e_rem_r)
            gcol = gsrc[:, lane:lane + 1]
            grow = gsrc_t[lane:lane + 1, :]
            bcol = beta[:, 4 * d + h:4 * d + h + 1]
            incl = (rowi >= coli) if d == 0 else (rowi <= coli)
            strict = (rowi > coli) if d == 0 else (rowi < coli)
            decay = jnp.exp(jnp.where(incl, gcol - grow, -1e30))
            ecol = esrc[:, lane:lane + 1]
            systems.append((d, hs, jnp.where(strict, -(kk * bcol) * decay, 0.0),
                            jnp.concatenate([vh * bcol, kh * (bcol * ecol)], axis=1)))
            kdt_ref[d, :, hs] = (kh * erem[:, lane:lane + 1]).T.astype(BF16)
            dl_ref[d, h:h + 1, :] = dl_all[4 * d + h:4 * d + h + 1, :]
            if emit:
                qg_ref[d, :, hs] = (qh * ecol).astype(BF16)
                qkm_ref[d, :, hs] = jnp.where(incl, qk * decay, 0.0).astype(BF16)

    rs = [jnp.where((rowi >> 1) == (coli >> 1), nmat, 0.0) for _, _, nmat, _ in systems]
    for lg in range(1, int(math.log2(C))):
        cross = ((rowi >> (lg + 1)) == (coli >> (lg + 1))) & ((rowi >> lg) != (coli >> lg))
        rbs = [r.astype(BF16) for r in rs]
        ms = []
        for (_, _, nmat, _), rb in zip(systems, rbs):
            ns = jnp.where(cross, nmat, 0.0)
            ms.append(ns + _mm(rb, ns.astype(BF16)))
        rs = [r + m + _mm(m.astype(BF16), rb) for r, m, rb in zip(rs, ms, rbs)]
    for (d, hs, _, rhs), r in zip(systems, rs):
        uw = rhs + _mm(r.astype(BF16), rhs.astype(BF16))
        u_ref[d, :, hs] = uw[:, :LANES].astype(BF16)
        w_ref[d, :, hs] = uw[:, LANES:].astype(BF16)


def _dn_prepare(qkv, ba, conv_w, alog, dtb, emit):
    b, t, cw = qkv.shape
    C = DN_CHUNK
    nchunks = t // C
    per = C // PACK_ROWS
    last = t // PACK_ROWS - 1
    seq_out = pl.BlockSpec((None, 2, C, DN_VW), lambda i, c: (i, 0, c, 0))
    seq_shape = jax.ShapeDtypeStruct((b, 2, t, DN_VW), BF16)
    out_specs = [seq_out, seq_out, seq_out, pl.BlockSpec((None, 2, None, DN_HEADS, LANES), lambda i, c: (i, 0, c, 0, 0))]
    out_shape = [seq_shape, seq_shape, seq_shape, jax.ShapeDtypeStruct((b, 2, nchunks, DN_HEADS, LANES), F32)]
    if emit:
        out_specs += [seq_out, seq_out]
        out_shape += [seq_shape, seq_shape]
    return pl.pallas_call(
        functools.partial(_dn_prep_body, nchunks=nchunks, emit=emit),
        grid=(b, nchunks),
        in_specs=[pl.BlockSpec((None, C, cw), lambda i, c: (i, c, 0)),
                  pl.BlockSpec((None, PACK_ROWS, cw), lambda i, c: (i, jnp.maximum(c * per - 1, 0), 0)),
                  pl.BlockSpec((None, PACK_ROWS, cw), lambda i, c: (i, jnp.minimum((c + 1) * per, last), 0)),
                  pl.BlockSpec((None, C, LANES), lambda i, c: (i, c, 0)),
                  _const_spec(conv_w.shape), _const_spec(alog.shape), _const_spec(dtb.shape)],
        out_specs=out_specs,
        out_shape=out_shape,
        compiler_params=_params(2),
        name="deltanet_prepare" if emit else "deltanet_prepare_ctx",
    )(qkv, qkv, qkv, ba, conv_w, alog, dtb)


def _dn_scan_body(uc_ref, wc_ref, kdc_ref, dlc_ref, ul_ref, wl_ref, kdl_ref, dll_ref, qg_ref, qkm_ref, o_ref, s_ref,
                  *, n_ctx, n_lat):
    d = pl.program_id(1)
    C = DN_CHUNK
    s_ref[...] = jnp.zeros_like(s_ref)

    def step(c, n, u_ref, w_ref, kdt_ref, dl_ref, emit):
        ce = jnp.where(d == 0, c, n - 1 - c)
        rows = pl.ds(pl.multiple_of(ce * C, C), C)
        heads = range(DN_HEADS)
        hs = [slice(h * LANES, (h + 1) * LANES) for h in heads]
        ss = [s_ref[h] for h in heads]
        sbs = [s.astype(BF16) for s in ss]
        ws = [_mm(w_ref[rows, hs[h]], sbs[h]) for h in heads]
        if emit:
            os = [_mm(qg_ref[rows, hs[h]], sbs[h]) for h in heads]
        vnbs = [(u_ref[rows, hs[h]].astype(F32) - ws[h]).astype(BF16) for h in heads]
        for h in heads:
            s_ref[h] = ss[h] * dl_ref[ce, h:h + 1, :] + _mm(kdt_ref[rows, hs[h]], vnbs[h])
        if emit:
            for h in heads:
                o_ref[rows, hs[h]] = (os[h] + _mm(qkm_ref[rows, hs[h]], vnbs[h])).astype(BF16)

    for c in range(n_ctx):
        step(c, n_ctx, uc_ref, wc_ref, kdc_ref, dlc_ref, False)

    def body(c, carry):
        step(c, n_lat, ul_ref, wl_ref, kdl_ref, dll_ref, True)
        return carry

    lax.fori_loop(0, n_lat, body, 0)


def _dn_scan(ctx_ops, lat_ops):
    uc, wc, kdc, dlc = ctx_ops
    ul, wl, kdl, dll, qg, qkm = lat_ops
    b, _, t, vw = ul.shape
    s = uc.shape[2]
    n_ctx, n_lat = s // DN_CHUNK, t // DN_CHUNK
    seq = lambda n: pl.BlockSpec((None, None, n, vw), lambda i, d: (i, d, 0, 0))
    dls = lambda n: pl.BlockSpec((None, None, n, DN_HEADS, LANES), lambda i, d: (i, d, 0, 0, 0))
    return pl.pallas_call(
        functools.partial(_dn_scan_body, n_ctx=n_ctx, n_lat=n_lat),
        grid=(b, 2),
        in_specs=[seq(s), seq(s), seq(s), dls(n_ctx), seq(t), seq(t), seq(t), dls(n_lat), seq(t), seq(t)],
        out_specs=seq(t),
        out_shape=jax.ShapeDtypeStruct((b, 2, t, vw), BF16),
        scratch_shapes=[pltpu.VMEM((DN_HEADS, DN_DK, DN_DV), F32)],
        compiler_params=_params(2),
        name="deltanet_scan",
    )(uc, wc, kdc, dlc, ul, wl, kdl, dll, qg, qkm)


def _merge_ffn_body(x_ref, om_ref, of_ref, ob_ref, z_ref, gate_ref, gt1_ref, sh2_ref, sc2_ref, gt2_ref,
                    gdn_ref, woa_ref, wod_ref, wout_ref, gpost_ref, gpre2_ref, w1_ref, w2_ref, gpost2_ref, out_ref):
    odn = of_ref[...].astype(F32) + ob_ref[...].astype(F32)
    parts = []
    for h in range(DN_HEADS):
        hs = slice(h * DN_DV, (h + 1) * DN_DV)
        parts.append(_rms(odn[:, hs], gdn_ref[...]) * _silu(z_ref[:, hs].astype(F32)))
    o_dn = jnp.concatenate(parts, axis=1).astype(BF16)
    ya = _mm(om_ref[...], woa_ref[...])
    yb = _mm(o_dn, wod_ref[...])
    mix = gate_ref[:, :D_MODEL].astype(F32) * ya + gate_ref[:, D_MODEL:].astype(F32) * yb
    y = _mm(mix.astype(BF16), wout_ref[...])
    x1 = x_ref[...] + gt1_ref[...] * _rms(y, gpost_ref[...])
    u2 = (_rms(x1, gpre2_ref[...]) * (1.0 + sc2_ref[...]) + sh2_ref[...]).astype(BF16)
    acc = jnp.zeros(x1.shape, F32)
    for j in range(FF_HIDDEN // FF_TILE):
        a = _mm(u2, w1_ref[:, j * FF_TILE:(j + 1) * FF_TILE])
        up = _mm(u2, w1_ref[:, FF_HIDDEN + j * FF_TILE:FF_HIDDEN + (j + 1) * FF_TILE])
        acc = acc + _mm((_silu(a) * up).astype(BF16), w2_ref[j * FF_TILE:(j + 1) * FF_TILE, :])
    out_ref[...] = x1 + gt2_ref[...] * _rms(acc, gpost2_ref[...])


def _merge_ffn(x2, seq, om, o_dn2, z, gates, mod3, gdn, woa, wod, wout, gpost, gpre2, w1, w2, gpost2):
    n, d = x2.shape
    tm = min(ROW_TILE, seq)
    tiles = seq // tm
    nb = n // seq
    row = lambda i: (i, 0)
    modspec = lambda k: pl.BlockSpec((None, 1, d), lambda i: (i // tiles, 0, k))
    dn_spec = lambda dirn: pl.BlockSpec((None, None, tm, DN_VW), lambda i: (i // tiles, dirn, i % tiles, 0))
    return pl.pallas_call(
        _merge_ffn_body,
        grid=(n // tm,),
        in_specs=[pl.BlockSpec((tm, d), row), pl.BlockSpec((tm, om.shape[1]), row), dn_spec(0), dn_spec(1),
                  pl.BlockSpec((tm, DN_VW), row), pl.BlockSpec((tm, 2 * d), row),
                  modspec(2), modspec(3), modspec(4), modspec(5),
                  _const_spec(gdn.shape), _const_spec(woa.shape), _const_spec(wod.shape), _const_spec(wout.shape),
                  _const_spec(gpost.shape), _const_spec(gpre2.shape), _const_spec(w1.shape), _const_spec(w2.shape),
                  _const_spec(gpost2.shape)],
        out_specs=pl.BlockSpec((tm, d), row),
        out_shape=jax.ShapeDtypeStruct((n, d), F32),
        compiler_params=_params(1),
        name="merge_ffn",
    )(x2, om, o_dn2, o_dn2, z, gates, mod3, mod3, mod3, mod3, gdn, woa, wod, wout, gpost, gpre2, w1, w2, gpost2)


def _rot_half(w):
    n = MLA_ROPE // 4
    return jnp.concatenate([-w[..., n:2 * n], w[..., 0:n], -w[..., 3 * n:4 * n], w[..., 2 * n:3 * n]], axis=-1)


def _rope_tables(rows):
    row = jnp.repeat(jnp.arange(rows, dtype=F32), GRID_W)
    col = jnp.tile(jnp.arange(GRID_W, dtype=F32), rows)
    n = MLA_ROPE // 4
    inv = ROPE_BASE ** (-jnp.arange(n, dtype=F32) / n)
    ar, ac = row[:, None] * inv, col[:, None] * inv
    cos = jnp.concatenate([jnp.cos(ar), jnp.cos(ar), jnp.cos(ac), jnp.cos(ac)], axis=-1)
    sin = jnp.concatenate([jnp.sin(ar), jnp.sin(ar), jnp.sin(ac), jnp.sin(ac)], axis=-1)
    return cos, sin


def _pad_tab(nope_val, rope, t):
    return jnp.concatenate([jnp.full((t, MLA_NOPE), nope_val, F32), rope,
                            jnp.zeros((t, LANES - MLA_NOPE - MLA_ROPE), F32)], axis=-1)


def kernel(x, c, ctx, c_ctx, w_mod, b_mod, g_pre_mix, g_post_mix, g_pre_ffn, g_post_ffn, w_in, g_q_lora, w_uq,
           g_kv_lora, w_ukv, w_o_mla, conv_qkv, a_log, dt_bias, g_dn_out, w_o_dn, w_out, w_ffn_in, w_ffn_out):
    B, T, D = x.shape
    S = ctx.shape[1]
    assert w_mod.shape[0] == 1 and D == D_MODEL
    assert T % max(ROW_TILE, DN_CHUNK, Q_TILE) == 0 and S % DN_CHUNK == 0 and T % GRID_W == 0
    L = 0
    row2 = lambda a: a[L].reshape(1, -1)

    mod_rows = -(-(B + 1) // 8) * 8
    cc = jnp.concatenate([c, c_ctx[None, :], jnp.zeros((mod_rows - B - 1, D), F32)], axis=0)
    mod3 = _modulation(cc, w_mod[L], b_mod[L]).reshape(mod_rows, 1, 6 * D)

    wi = w_in[L]
    o_cq, o_ckv, o_kr = 0, Q_LORA, Q_LORA + KV_LORA
    o_qkv = o_kr + MLA_ROPE
    o_z = o_qkv + 2 * DN_QK + DN_VW
    o_beta = o_z + DN_VW
    o_gate = o_beta + 4 * DN_HEADS
    w_kr = wi[:, o_kr:o_qkv]
    zc = lambda n: jnp.zeros((D, n), F32)
    w_pack = jnp.concatenate([
        wi[:, o_cq:o_kr],
        wi[:, o_beta:o_gate], zc(MLA_NOPE - 4 * DN_HEADS), w_kr, zc(LANES - MLA_NOPE - MLA_ROPE),
        zc(MLA_NOPE), _rot_half(w_kr), zc(LANES - MLA_NOPE - MLA_ROPE),
        wi[:, o_qkv:o_beta], wi[:, o_gate:]], axis=1).astype(BF16)
    assert w_pack.shape[1] == C_END

    dq = MLA_NOPE + MLA_ROPE
    wq = w_uq[L].reshape(Q_LORA, MLA_HEADS, dq)
    zq = jnp.zeros((Q_LORA, MLA_HEADS, LANES - dq), F32)
    wq1 = jnp.concatenate([wq, zq], axis=-1).reshape(Q_LORA, MLA_HEADS * LANES).astype(BF16)
    wq2 = jnp.concatenate([jnp.zeros((Q_LORA, MLA_HEADS, MLA_NOPE), F32), _rot_half(wq[..., MLA_NOPE:]), zq],
                          axis=-1).reshape(Q_LORA, MLA_HEADS * LANES).astype(BF16)
    wkv = w_ukv[L].reshape(KV_LORA, MLA_HEADS, MLA_NOPE + MLA_V)
    wk = jnp.concatenate([wkv[..., :MLA_NOPE], jnp.zeros((KV_LORA, MLA_HEADS, LANES - MLA_NOPE), F32)],
                         axis=-1).reshape(KV_LORA, MLA_HEADS * LANES).astype(BF16)
    wv = wkv[..., MLA_NOPE:].reshape(KV_LORA, MLA_HEADS * MLA_V).astype(BF16)

    cos, sin = _rope_tables(T // GRID_W)
    qs = dq ** -0.5 * LOG2E
    tabs_lat = (_pad_tab(qs, cos * qs, T), _pad_tab(0.0, sin * qs, T), _pad_tab(0.0, cos, T), _pad_tab(0.0, sin, T))
    ones = jnp.ones((S, MLA_ROPE), F32)
    no_rot = _pad_tab(0.0, 0.0 * ones, S)
    tabs_ctx = (no_rot, no_rot, _pad_tab(0.0, ones, S), no_rot)

    x2 = x.reshape(B * T, D)
    c2 = ctx.reshape(B * S, D)
    gpre, gq, gkv = row2(g_pre_mix), row2(g_q_lora), row2(g_kv_lora)
    q, k, v, qkv, ba, z, gates = _input_projection(
        x2, mod3, T, T, w_pack, gpre, gq, wq1, wq2, gkv, wk, wv, tabs_lat, True)
    k_c, v_c, qkv_c, ba_c = _input_projection(
        c2, mod3[B:B + 1], B * S, S, w_pack[:, :C_CTX_END], gpre, gq, wq1, wq2, gkv, wk, wv, tabs_ctx, False)

    r3 = lambda a, n: a.reshape(B, n, a.shape[-1])
    o_mla = _attention(r3(q, T), r3(k, T), r3(k_c, S), r3(v, T), r3(v_c, S))

    lane_pad = lambda a, fill: jnp.pad(a.astype(F32).reshape(1, -1), ((0, 0), (8, LANES - 16)), constant_values=fill)
    alog = lane_pad(a_log[L], -1e30)
    dtb = lane_pad(dt_bias[L], 0.0)
    cw = conv_qkv[L].astype(F32)
    ctx_ops = _dn_prepare(r3(qkv_c, S), r3(ba_c, S), cw, alog, dtb, False)
    lat_ops = _dn_prepare(r3(qkv, T), r3(ba, T), cw, alog, dtb, True)
    o_dn2 = _dn_scan(ctx_ops, lat_ops)

    out = _merge_ffn(x2, T, o_mla.reshape(B * T, -1), o_dn2, z, gates, mod3, row2(g_dn_out),
                     w_o_mla[L].astype(BF16), w_o_dn[L].astype(BF16), w_out[L].astype(BF16), row2(g_post_mix),
                     row2(g_pre_ffn), w_ffn_in[L].astype(BF16), w_ffn_out[L].astype(BF16), row2(g_post_ffn))
    return out.reshape(B, T, D)
```

```python
import functools
import math

import jax
import jax.numpy as jnp
from jax import lax
from jax.experimental import pallas as pl
from jax.experimental.pallas import tpu as pltpu

F32 = jnp.float32
BF16 = jnp.bfloat16

D_MODEL = 1024
GRID_W = 64
MLA_HEADS = 8
MLA_NOPE = 64
MLA_ROPE = 32
MLA_V = 64
Q_LORA = 512
KV_LORA = 256
ROPE_BASE = 10000.0
DN_HEADS = 4
DN_DK = 128
DN_DV = 128
DN_QK = DN_HEADS * DN_DK
DN_VW = DN_HEADS * DN_DV
FF_HIDDEN = 2816
EPS = 1e-6

LANES = 128
DN_CHUNK = 128
ROW_TILE = 512
Q_TILE = 512
V_GROUP = 4
SCAN_BLOCK = 512
FF_TILE = 256
PACK_ROWS = 16
VMEM_LIMIT = 56 * 1024 * 1024
LOG2E = math.log2(math.e)

C_CQ = 0
C_CKV = C_CQ + Q_LORA
C_M1 = C_CKV + KV_LORA
C_M2 = C_M1 + LANES
C_QKV = C_M2 + LANES
C_Z = C_QKV + 2 * DN_QK + DN_VW
C_GATE = C_Z + DN_VW
C_END = C_GATE + 2 * D_MODEL
C_CTX_END = C_Z


def _mm(a, b):
    return jnp.dot(a, b, preferred_element_type=F32)


def _mm_nt(a, b):
    return lax.dot_general(a, b, (((1,), (1,)), ((), ())), preferred_element_type=F32)


def _sigmoid(x):
    return 1.0 / (1.0 + jnp.exp(-x))


def _silu(x):
    return x * _sigmoid(x)


def _rms(x, g):
    return x * lax.rsqrt(jnp.mean(x * x, axis=-1, keepdims=True) + EPS) * g


def _params(n_axes):
    return pltpu.CompilerParams(dimension_semantics=("arbitrary",) * n_axes, vmem_limit_bytes=VMEM_LIMIT)


def _const_spec(shape):
    nd = len(shape)
    return pl.BlockSpec(shape, lambda *_: (0,) * nd, pipeline_mode=pl.Buffered(1))


def _mod_body(c_ref, w_ref, b_ref, o_ref):
    s = _silu(c_ref[...])
    o_ref[...] = jnp.dot(s, w_ref[...], preferred_element_type=F32, precision=lax.Precision.HIGHEST) + b_ref[...]


def _modulation(cc, w_mod, b_mod):
    rows, d = cc.shape
    n = w_mod.shape[1]
    tn = 1024
    return pl.pallas_call(
        _mod_body,
        grid=(n // tn,),
        in_specs=[pl.BlockSpec((rows, d), lambda j: (0, 0)),
                  pl.BlockSpec((d, tn), lambda j: (0, j)),
                  pl.BlockSpec((1, tn), lambda j: (0, j))],
        out_specs=pl.BlockSpec((rows, tn), lambda j: (0, j)),
        out_shape=jax.ShapeDtypeStruct((rows, n), F32),
        compiler_params=_params(1),
        name="modulation",
    )(cc, w_mod, b_mod.reshape(1, n))


def _inproj_body(x_ref, sh_ref, sc_ref, gpre_ref, w_ref, gq_ref, wq1_ref, wq2_ref, gkv_ref, wk_ref, wv_ref,
                 cq_tab, sq_tab, ck_tab, sk_tab, *outs, full):
    if full:
        q_ref, k_ref, v_ref, qkv_ref, ba_ref, z_ref, gate_ref = outs
    else:
        k_ref, v_ref, qkv_ref, ba_ref = outs
    u = (_rms(x_ref[...], gpre_ref[...]) * (1.0 + sc_ref[...]) + sh_ref[...]).astype(BF16)

    if full:
        nq = _rms(_mm(u, w_ref[:, C_CQ:C_CKV]), gq_ref[...]).astype(BF16)
        qa = _mm(nq, wq1_ref[...])
        qb = _mm(nq, wq2_ref[...])
        cq = cq_tab[...]
        sq = sq_tab[...]
        for h in range(MLA_HEADS):
            sl = slice(h * LANES, (h + 1) * LANES)
            q_ref[:, sl] = (qa[:, sl] * cq + qb[:, sl] * sq).astype(BF16)

    nkv = _rms(_mm(u, w_ref[:, C_CKV:C_M1]), gkv_ref[...]).astype(BF16)
    m1 = _mm(u, w_ref[:, C_M1:C_M2])
    m2 = _mm(u, w_ref[:, C_M2:C_QKV])
    ba_ref[...] = m1
    k_rope = m1 * ck_tab[...] + m2 * sk_tab[...]
    kn = _mm(nkv, wk_ref[...])
    for h in range(MLA_HEADS):
        sl = slice(h * LANES, (h + 1) * LANES)
        k_ref[:, sl] = (kn[:, sl] + k_rope).astype(BF16)
    v_ref[...] = _mm(nkv, wv_ref[...]).astype(BF16)
    qkv_ref[...] = _mm(u, w_ref[:, C_QKV:C_Z]).astype(BF16)
    if full:
        z_ref[...] = _mm(u, w_ref[:, C_Z:C_GATE]).astype(BF16)
        gate_ref[...] = _sigmoid(_mm(u, w_ref[:, C_GATE:C_END])).astype(BF16)


def _input_projection(x2, mod3, mod_period, tab_period, w, gpre, gq, wq1, wq2, gkv, wk, wv, tabs, full):
    n, d = x2.shape
    tm = min(ROW_TILE, tab_period)
    mod_tiles = mod_period // tm
    tab_tiles = tab_period // tm
    row = lambda i: (i, 0)
    tab = lambda i: (i % tab_tiles, 0)
    in_specs = [
        pl.BlockSpec((tm, d), row),
        pl.BlockSpec((None, 1, d), lambda i: (i // mod_tiles, 0, 0)),
        pl.BlockSpec((None, 1, d), lambda i: (i // mod_tiles, 0, 1)),
        _const_spec(gpre.shape), _const_spec(w.shape), _const_spec(gq.shape), _const_spec(wq1.shape),
        _const_spec(wq2.shape), _const_spec(gkv.shape), _const_spec(wk.shape), _const_spec(wv.shape),
        pl.BlockSpec((tm, LANES), tab), pl.BlockSpec((tm, LANES), tab), pl.BlockSpec((tm, LANES), tab),
        pl.BlockSpec((tm, LANES), tab),
    ]
    kw = MLA_HEADS * LANES
    vw = MLA_HEADS * MLA_V
    qkvw = 2 * DN_QK + DN_VW
    outs = [(kw, BF16), (vw, BF16), (qkvw, BF16), (LANES, F32)]
    if full:
        outs = [(kw, BF16)] + outs + [(DN_VW, BF16), (2 * D_MODEL, BF16)]
    return pl.pallas_call(
        functools.partial(_inproj_body, full=full),
        grid=(n // tm,),
        in_specs=in_specs,
        out_specs=[pl.BlockSpec((tm, c), row) for c, _ in outs],
        out_shape=[jax.ShapeDtypeStruct((n, c), t) for c, t in outs],
        compiler_params=_params(1),
        name="input_projection" if full else "input_projection_ctx",
    )(x2, mod3, mod3, gpre, w, gq, wq1, wq2, gkv, wk, wv, *tabs)


def _attn_body(q_ref, kl_ref, kc_ref, vl_ref, vc_ref, o_ref):
    for h in range(MLA_HEADS):
        sl = slice(h * LANES, (h + 1) * LANES)
        sv = slice(h * MLA_V, (h + 1) * MLA_V)
        qh = q_ref[:, sl]
        s1 = _mm_nt(qh, kl_ref[:, sl])
        s2 = _mm_nt(qh, kc_ref[:, sl])
        m = jnp.maximum(jnp.max(s1, axis=-1, keepdims=True), jnp.max(s2, axis=-1, keepdims=True))
        p1 = jnp.exp2(s1 - m)
        p2 = jnp.exp2(s2 - m)
        l = jnp.sum(p1, axis=-1, keepdims=True) + jnp.sum(p2, axis=-1, keepdims=True)
        g0 = h // V_GROUP * V_GROUP * MLA_V
        sg = slice(g0, g0 + V_GROUP * MLA_V)
        og = _mm(p1.astype(BF16), vl_ref[:, sg]) + _mm(p2.astype(BF16), vc_ref[:, sg])
        o = og[:, h % V_GROUP * MLA_V:(h % V_GROUP + 1) * MLA_V]
        o_ref[:, sv] = (o / l).astype(BF16)


def _attention(q, kl, kc, vl, vc):
    b, t, kw = q.shape
    s = kc.shape[1]
    vw = vl.shape[2]
    tq = min(Q_TILE, t)
    return pl.pallas_call(
        _attn_body,
        grid=(b, t // tq),
        in_specs=[pl.BlockSpec((None, tq, kw), lambda i, j: (i, j, 0)),
                  pl.BlockSpec((None, t, kw), lambda i, j: (i, 0, 0)),
                  pl.BlockSpec((None, s, kw), lambda i, j: (i, 0, 0)),
                  pl.BlockSpec((None, t, vw), lambda i, j: (i, 0, 0)),
                  pl.BlockSpec((None, s, vw), lambda i, j: (i, 0, 0))],
        out_specs=pl.BlockSpec((None, tq, vw), lambda i, j: (i, j, 0)),
        out_shape=jax.ShapeDtypeStruct((b, t, vw), BF16),
        compiler_params=_params(2),
        name="attention",
    )(q, kl, kc, vl, vc)


def _dn_prep_body(x_ref, xp_ref, xn_ref, ba_ref, cw_ref, alog_ref, dtb_ref, *outs, nchunks, emit):
    if emit:
        u_ref, w_ref, kdt_ref, dl_ref, qg_ref, qkm_ref = outs
    else:
        u_ref, w_ref, kdt_ref, dl_ref = outs
    c = pl.program_id(1)
    C = DN_CHUNK
    has_prev = jnp.where(c > 0, 1.0, 0.0)
    has_next = jnp.where(c < nchunks - 1, 1.0, 0.0)
    rowi = lax.broadcasted_iota(jnp.int32, (C, C), 0)
    coli = lax.broadcasted_iota(jnp.int32, (C, C), 1)

    def conv_slab(s):
        sl = slice(s * LANES, (s + 1) * LANES)
        x = x_ref[:, sl].astype(F32)
        prev = xp_ref[PACK_ROWS - 1:PACK_ROWS, sl].astype(F32) * has_prev
        nxt = xn_ref[0:1, sl].astype(F32) * has_next
        xd = jnp.where(rowi == 0, prev, pltpu.roll(x, 1, 0))
        xu = jnp.where(rowi == C - 1, nxt, pltpu.roll(x, C - 1, 0))
        return _silu(xd * cw_ref[0:1, sl] + x * cw_ref[1:2, sl] + xu * cw_ref[2:3, sl])

    def l2n(a):
        return a * lax.rsqrt(jnp.sum(a * a, axis=-1, keepdims=True) + EPS)

    bg = ba_ref[...]
    beta = _sigmoid(bg)
    xs = bg + dtb_ref[...]
    softplus = jnp.maximum(xs, 0.0) + jnp.log(1.0 + jnp.exp(-jnp.abs(xs)))
    g = -jnp.exp(alog_ref[...]) * softplus
    gc = g
    sft = 1
    while sft < C:
        gc = gc + jnp.where(rowi >= sft, pltpu.roll(gc, sft, 0), 0.0)
        sft *= 2
    gtot = gc[C - 1:C, :]
    gcr = gtot - gc + g
    gc_t = gc.T
    gcr_t = gcr.T
    e_gc = jnp.exp(gc)
    e_gcr = jnp.exp(gcr)
    e_rem = jnp.exp(gtot - gc)
    e_rem_r = jnp.exp(gtot - gcr)
    dl_all = jnp.exp(jnp.broadcast_to(gc_t[8:16, C - 1:C], (8, LANES)))
    scale = DN_DK ** -0.5

    systems = []
    for h in range(DN_HEADS):
        hs = slice(h * LANES, (h + 1) * LANES)
        qh = l2n(conv_slab(h)) * scale
        kh = l2n(conv_slab(DN_HEADS + h))
        vh = conv_slab(2 * DN_HEADS + h)
        khb = kh.astype(BF16)
        kk = _mm_nt(khb, khb)
        if emit:
            qk = _mm_nt(qh.astype(BF16), khb)
        for d in range(2):
            lane = 8 + 4 * d + h
            gsrc, gsrc_t, esrc, erem = (gc, gc_t, e_gc, e_rem) if d == 0 else (gcr, gcr_t, e_gcr, e_rem_r)
            gcol = gsrc[:, lane:lane + 1]
            grow = gsrc_t[lane:lane + 1, :]
            bcol = beta[:, 4 * d + h:4 * d + h + 1]
            incl = (rowi >= coli) if d == 0 else (rowi <= coli)
            strict = (rowi > coli) if d == 0 else (rowi < coli)
            decay = jnp.exp(jnp.where(incl, gcol - grow, -1e30))
            ecol = esrc[:, lane:lane + 1]
            systems.append((d, hs, jnp.where(strict, -(kk * bcol) * decay, 0.0),
                            jnp.concatenate([vh * bcol, kh * (bcol * ecol)], axis=1)))
            kdt_ref[d, :, hs] = (kh * erem[:, lane:lane + 1]).T.astype(BF16)
            dl_ref[d, h:h + 1, :] = dl_all[4 * d + h:4 * d + h + 1, :]
            if emit:
                qg_ref[d, :, hs] = (qh * ecol).astype(BF16)
                qkm_ref[d, :, hs] = jnp.where(incl, qk * decay, 0.0).astype(BF16)

    rs = [jnp.where((rowi >> 1) == (coli >> 1), nmat, 0.0) for _, _, nmat, _ in systems]
    for lg in range(1, int(math.log2(C))):
        cross = ((rowi >> (lg + 1)) == (coli >> (lg + 1))) & ((rowi >> lg) != (coli >> lg))
        rbs = [r.astype(BF16) for r in rs]
        ms = []
        for (_, _, nmat, _), rb in zip(systems, rbs):
            ns = jnp.where(cross, nmat, 0.0)
            ms.append(ns + _mm(rb, ns.astype(BF16)))
        rs = [r + m + _mm(m.astype(BF16), rb) for r, m, rb in zip(rs, ms, rbs)]
    for (d, hs, _, rhs), r in zip(systems, rs):
        uw = rhs + _mm(r.astype(BF16), rhs.astype(BF16))
        u_ref[d, :, hs] = uw[:, :LANES].astype(BF16)
        w_ref[d, :, hs] = uw[:, LANES:].astype(BF16)


def _dn_prepare(qkv, ba, conv_w, alog, dtb, emit):
    b, t, cw = qkv.shape
    C = DN_CHUNK
    nchunks = t // C
    per = C // PACK_ROWS
    last = t // PACK_ROWS - 1
    seq_out = pl.BlockSpec((None, 2, C, DN_VW), lambda i, c: (i, 0, c, 0))
    seq_shape = jax.ShapeDtypeStruct((b, 2, t, DN_VW), BF16)
    out_specs = [seq_out, seq_out, seq_out, pl.BlockSpec((None, 2, None, DN_HEADS, LANES), lambda i, c: (i, 0, c, 0, 0))]
    out_shape = [seq_shape, seq_shape, seq_shape, jax.ShapeDtypeStruct((b, 2, nchunks, DN_HEADS, LANES), F32)]
    if emit:
        out_specs += [seq_out, seq_out]
        out_shape += [seq_shape, seq_shape]
    return pl.pallas_call(
        functools.partial(_dn_prep_body, nchunks=nchunks, emit=emit),
        grid=(b, nchunks),
        in_specs=[pl.BlockSpec((None, C, cw), lambda i, c: (i, c, 0)),
                  pl.BlockSpec((None, PACK_ROWS, cw), lambda i, c: (i, jnp.maximum(c * per - 1, 0), 0)),
                  pl.BlockSpec((None, PACK_ROWS, cw), lambda i, c: (i, jnp.minimum((c + 1) * per, last), 0)),
                  pl.BlockSpec((None, C, LANES), lambda i, c: (i, c, 0)),
                  _const_spec(conv_w.shape), _const_spec(alog.shape), _const_spec(dtb.shape)],
        out_specs=out_specs,
        out_shape=out_shape,
        compiler_params=_params(2),
        name="deltanet_prepare" if emit else "deltanet_prepare_ctx",
    )(qkv, qkv, qkv, ba, conv_w, alog, dtb)


def _dn_scan_body(*refs, n_ctx, per_block):
    ctx_f, ctx_b = refs[0:4], refs[4:8]
    lat_f, lat_b = refs[8:14], refs[14:20]
    of_ref, ob_ref, s_ref = refs[20:23]
    C = DN_CHUNK
    heads = range(DN_HEADS)
    hs = [slice(h * LANES, (h + 1) * LANES) for h in heads]

    def step(srcs):
        chains = [(d, lc, ops, o_ref, h) for d, lc, ops, o_ref in srcs for h in heads]
        rows = lambda lc: slice(lc * C, (lc + 1) * C)
        ss = [s_ref[d, h] for d, _, _, _, h in chains]
        sbs = [s.astype(BF16) for s in ss]
        ws = [_mm(ops[1][rows(lc), hs[h]], sb) for (_, lc, ops, _, h), sb in zip(chains, sbs)]
        os = [_mm(ops[4][rows(lc), hs[h]], sb) if o_ref is not None else None
              for (_, lc, ops, o_ref, h), sb in zip(chains, sbs)]
        vnbs = [(ops[0][rows(lc), hs[h]].astype(F32) - w).astype(BF16) for (_, lc, ops, _, h), w in zip(chains, ws)]
        for (d, lc, ops, _, h), s, vnb in zip(chains, ss, vnbs):
            s_ref[d, h] = s * ops[3][lc, h:h + 1, :] + _mm(ops[2][rows(lc), hs[h]], vnb)
        for (_, lc, ops, o_ref, h), o, vnb in zip(chains, os, vnbs):
            if o_ref is not None:
                o_ref[rows(lc), hs[h]] = (o + _mm(ops[5][rows(lc), hs[h]], vnb)).astype(BF16)

    @pl.when(pl.program_id(1) == 0)
    def _():
        s_ref[...] = jnp.zeros_like(s_ref)
        for c in range(n_ctx):
            step([(0, c, ctx_f, None), (1, n_ctx - 1 - c, ctx_b, None)])

    for c in range(per_block):
        step([(0, c, lat_f, of_ref), (1, per_block - 1 - c, lat_b, ob_ref)])


def _dn_scan(ctx_ops, lat_ops):
    b, _, t, vw = lat_ops[0].shape
    s = ctx_ops[0].shape[2]
    sb = min(SCAN_BLOCK, t)
    per_block, nblk, n_ctx = sb // DN_CHUNK, t // sb, s // DN_CHUNK
    blk = lambda d: (lambda i, j: (i, d, j if d == 0 else nblk - 1 - j, 0))
    seq_c = lambda d: pl.BlockSpec((None, None, s, vw), lambda i, j: (i, d, 0, 0))
    dl_c = lambda d: pl.BlockSpec((None, None, n_ctx, DN_HEADS, LANES), lambda i, j: (i, d, 0, 0, 0))
    seq_l = lambda d: pl.BlockSpec((None, None, sb, vw), blk(d))
    dl_l = lambda d: pl.BlockSpec((None, None, per_block, DN_HEADS, LANES), lambda i, j: blk(d)(i, j) + (0,))
    ctx_specs = lambda d: [seq_c(d), seq_c(d), seq_c(d), dl_c(d)]
    lat_specs = lambda d: [seq_l(d), seq_l(d), seq_l(d), dl_l(d), seq_l(d), seq_l(d)]
    out_f = pl.BlockSpec((None, sb, vw), lambda i, j: (i, j, 0))
    out_b = pl.BlockSpec((None, sb, vw), lambda i, j: (i, nblk - 1 - j, 0))
    out_shape = jax.ShapeDtypeStruct((b, t, vw), BF16)
    return pl.pallas_call(
        functools.partial(_dn_scan_body, n_ctx=n_ctx, per_block=per_block),
        grid=(b, nblk),
        in_specs=ctx_specs(0) + ctx_specs(1) + lat_specs(0) + lat_specs(1),
        out_specs=[out_f, out_b],
        out_shape=[out_shape, out_shape],
        scratch_shapes=[pltpu.VMEM((2, DN_HEADS, DN_DK, DN_DV), F32)],
        compiler_params=_params(2),
        name="deltanet_scan",
    )(*ctx_ops, *ctx_ops, *lat_ops, *lat_ops)


def _merge_ffn_body(x_ref, om_ref, of_ref, ob_ref, z_ref, gate_ref, gt1_ref, sh2_ref, sc2_ref, gt2_ref,
                    gdn_ref, woa_ref, wod_ref, wout_ref, gpost_ref, gpre2_ref, w1_ref, w2_ref, gpost2_ref, out_ref):
    odn = of_ref[...].astype(F32) + ob_ref[...].astype(F32)
    parts = []
    for h in range(DN_HEADS):
        hs = slice(h * DN_DV, (h + 1) * DN_DV)
        parts.append(_rms(odn[:, hs], gdn_ref[...]) * _silu(z_ref[:, hs].astype(F32)))
    o_dn = jnp.concatenate(parts, axis=1).astype(BF16)
    ya = _mm(om_ref[...], woa_ref[...])
    yb = _mm(o_dn, wod_ref[...])
    mix = gate_ref[:, :D_MODEL].astype(F32) * ya + gate_ref[:, D_MODEL:].astype(F32) * yb
    y = _mm(mix.astype(BF16), wout_ref[...])
    x1 = x_ref[...] + gt1_ref[...] * _rms(y, gpost_ref[...])
    u2 = (_rms(x1, gpre2_ref[...]) * (1.0 + sc2_ref[...]) + sh2_ref[...]).astype(BF16)
    acc = jnp.zeros(x1.shape, F32)
    for j in range(FF_HIDDEN // FF_TILE):
        a = _mm(u2, w1_ref[:, j * FF_TILE:(j + 1) * FF_TILE])
        up = _mm(u2, w1_ref[:, FF_HIDDEN + j * FF_TILE:FF_HIDDEN + (j + 1) * FF_TILE])
        acc = acc + _mm((_silu(a) * up).astype(BF16), w2_ref[j * FF_TILE:(j + 1) * FF_TILE, :])
    out_ref[...] = x1 + gt2_ref[...] * _rms(acc, gpost2_ref[...])


def _merge_ffn(x2, seq, om, o_f, o_b, z, gates, mod3, gdn, woa, wod, wout, gpost, gpre2, w1, w2, gpost2):
    n, d = x2.shape
    tm = min(ROW_TILE, seq)
    tiles = seq // tm
    row = lambda i: (i, 0)
    modspec = lambda k: pl.BlockSpec((None, 1, d), lambda i: (i // tiles, 0, k))
    dn_spec = pl.BlockSpec((tm, DN_VW), row)
    return pl.pallas_call(
        _merge_ffn_body,
        grid=(n // tm,),
        in_specs=[pl.BlockSpec((tm, d), row), pl.BlockSpec((tm, om.shape[1]), row), dn_spec, dn_spec,
                  pl.BlockSpec((tm, DN_VW), row), pl.BlockSpec((tm, 2 * d), row),
                  modspec(2), modspec(3), modspec(4), modspec(5),
                  _const_spec(gdn.shape), _const_spec(woa.shape), _const_spec(wod.shape), _const_spec(wout.shape),
                  _const_spec(gpost.shape), _const_spec(gpre2.shape), _const_spec(w1.shape), _const_spec(w2.shape),
                  _const_spec(gpost2.shape)],
        out_specs=pl.BlockSpec((tm, d), row),
        out_shape=jax.ShapeDtypeStruct((n, d), F32),
        compiler_params=_params(1),
        name="merge_ffn",
    )(x2, om, o_f, o_b, z, gates, mod3, mod3, mod3, mod3, gdn, woa, wod, wout, gpost, gpre2, w1, w2, gpost2)


def _rot_half(w):
    n = MLA_ROPE // 4
    return jnp.concatenate([-w[..., n:2 * n], w[..., 0:n], -w[..., 3 * n:4 * n], w[..., 2 * n:3 * n]], axis=-1)


def _rope_tables(rows):
    row = jnp.repeat(jnp.arange(rows, dtype=F32), GRID_W)
    col = jnp.tile(jnp.arange(GRID_W, dtype=F32), rows)
    n = MLA_ROPE // 4
    inv = ROPE_BASE ** (-jnp.arange(n, dtype=F32) / n)
    ar, ac = row[:, None] * inv, col[:, None] * inv
    cos = jnp.concatenate([jnp.cos(ar), jnp.cos(ar), jnp.cos(ac), jnp.cos(ac)], axis=-1)
    sin = jnp.concatenate([jnp.sin(ar), jnp.sin(ar), jnp.sin(ac), jnp.sin(ac)], axis=-1)
    return cos, sin


def _pad_tab(nope_val, rope, t):
    return jnp.concatenate([jnp.full((t, MLA_NOPE), nope_val, F32), rope,
                            jnp.zeros((t, LANES - MLA_NOPE - MLA_ROPE), F32)], axis=-1)


def kernel(x, c, ctx, c_ctx, w_mod, b_mod, g_pre_mix, g_post_mix, g_pre_ffn, g_post_ffn, w_in, g_q_lora, w_uq,
           g_kv_lora, w_ukv, w_o_mla, conv_qkv, a_log, dt_bias, g_dn_out, w_o_dn, w_out, w_ffn_in, w_ffn_out):
    B, T, D = x.shape
    S = ctx.shape[1]
    assert w_mod.shape[0] == 1 and D == D_MODEL
    assert T % max(ROW_TILE, DN_CHUNK, Q_TILE) == 0 and S % DN_CHUNK == 0 and T % GRID_W == 0
    L = 0
    row2 = lambda a: a[L].reshape(1, -1)

    mod_rows = -(-(B + 1) // 8) * 8
    cc = jnp.concatenate([c, c_ctx[None, :], jnp.zeros((mod_rows - B - 1, D), F32)], axis=0)
    mod3 = _modulation(cc, w_mod[L], b_mod[L]).reshape(mod_rows, 1, 6 * D)

    wi = w_in[L]
    o_cq, o_ckv, o_kr = 0, Q_LORA, Q_LORA + KV_LORA
    o_qkv = o_kr + MLA_ROPE
    o_z = o_qkv + 2 * DN_QK + DN_VW
    o_beta = o_z + DN_VW
    o_gate = o_beta + 4 * DN_HEADS
    w_kr = wi[:, o_kr:o_qkv]
    zc = lambda n: jnp.zeros((D, n), F32)
    w_pack = jnp.concatenate([
        wi[:, o_cq:o_kr],
        wi[:, o_beta:o_gate], zc(MLA_NOPE - 4 * DN_HEADS), w_kr, zc(LANES - MLA_NOPE - MLA_ROPE),
        zc(MLA_NOPE), _rot_half(w_kr), zc(LANES - MLA_NOPE - MLA_ROPE),
        wi[:, o_qkv:o_beta], wi[:, o_gate:]], axis=1).astype(BF16)
    assert w_pack.shape[1] == C_END

    dq = MLA_NOPE + MLA_ROPE
    wq = w_uq[L].reshape(Q_LORA, MLA_HEADS, dq)
    zq = jnp.zeros((Q_LORA, MLA_HEADS, LANES - dq), F32)
    wq1 = jnp.concatenate([wq, zq], axis=-1).reshape(Q_LORA, MLA_HEADS * LANES).astype(BF16)
    wq2 = jnp.concatenate([jnp.zeros((Q_LORA, MLA_HEADS, MLA_NOPE), F32), _rot_half(wq[..., MLA_NOPE:]), zq],
                          axis=-1).reshape(Q_LORA, MLA_HEADS * LANES).astype(BF16)
    wkv = w_ukv[L].reshape(KV_LORA, MLA_HEADS, MLA_NOPE + MLA_V)
    wk = jnp.concatenate([wkv[..., :MLA_NOPE], jnp.zeros((KV_LORA, MLA_HEADS, LANES - MLA_NOPE), F32)],
                         axis=-1).reshape(KV_LORA, MLA_HEADS * LANES).astype(BF16)
    wv = wkv[..., MLA_NOPE:].reshape(KV_LORA, MLA_HEADS * MLA_V).astype(BF16)

    cos, sin = _rope_tables(T // GRID_W)
    qs = dq ** -0.5 * LOG2E
    tabs_lat = (_pad_tab(qs, cos * qs, T), _pad_tab(0.0, sin * qs, T), _pad_tab(0.0, cos, T), _pad_tab(0.0, sin, T))
    ones = jnp.ones((S, MLA_ROPE), F32)
    no_rot = _pad_tab(0.0, 0.0 * ones, S)
    tabs_ctx = (no_rot, no_rot, _pad_tab(0.0, ones, S), no_rot)

    x2 = x.reshape(B * T, D)
    c2 = ctx.reshape(B * S, D)
    gpre, gq, gkv = row2(g_pre_mix), row2(g_q_lora), row2(g_kv_lora)
    q, k, v, qkv, ba, z, gates = _input_projection(
        x2, mod3, T, T, w_pack, gpre, gq, wq1, wq2, gkv, wk, wv, tabs_lat, True)
    k_c, v_c, qkv_c, ba_c = _input_projection(
        c2, mod3[B:B + 1], B * S, S, w_pack[:, :C_CTX_END], gpre, gq, wq1, wq2, gkv, wk, wv, tabs_ctx, False)

    r3 = lambda a, n: a.reshape(B, n, a.shape[-1])
    o_mla = _attention(r3(q, T), r3(k, T), r3(k_c, S), r3(v, T), r3(v_c, S))

    lane_pad = lambda a, fill: jnp.pad(a.astype(F32).reshape(1, -1), ((0, 0), (8, LANES - 16)), constant_values=fill)
    alog = lane_pad(a_log[L], -1e30)
    dtb = lane_pad(dt_bias[L], 0.0)
    cw = conv_qkv[L].astype(F32)
    ctx_ops = _dn_prepare(r3(qkv_c, S), r3(ba_c, S), cw, alog, dtb, False)
    lat_ops = _dn_prepare(r3(qkv, T), r3(ba, T), cw, alog, dtb, True)
    o_f, o_b = _dn_scan(ctx_ops, lat_ops)

    out = _merge_ffn(x2, T, o_mla.reshape(B * T, -1), o_f.reshape(B * T, -1), o_b.reshape(B * T, -1), z, gates, mod3,
                     row2(g_dn_out),
                     w_o_mla[L].astype(BF16), w_o_dn[L].astype(BF16), w_out[L].astype(BF16), row2(g_post_mix),
                     row2(g_pre_ffn), w_ffn_in[L].astype(BF16), w_ffn_out[L].astype(BF16), row2(g_post_ffn))
    return out.reshape(B, T, D)
```

```python
import functools
import math

import jax
import jax.numpy as jnp
from jax import lax
from jax.experimental import pallas as pl
from jax.experimental.pallas import tpu as pltpu

F32 = jnp.float32
BF16 = jnp.bfloat16

D_MODEL = 1024
GRID_W = 64
MLA_HEADS = 8
MLA_NOPE = 64
MLA_ROPE = 32
MLA_V = 64
Q_LORA = 512
KV_LORA = 256
ROPE_BASE = 10000.0
DN_HEADS = 4
DN_DK = 128
DN_DV = 128
DN_QK = DN_HEADS * DN_DK
DN_VW = DN_HEADS * DN_DV
FF_HIDDEN = 2816
EPS = 1e-6

LANES = 128
DN_CHUNK = 128
ROW_TILE = 512
Q_TILE = 512
V_GROUP = 4
SCAN_BLOCK = 512
PREP_CHUNKS = 2
FF_TILE = 256
PACK_ROWS = 16
VMEM_LIMIT = 56 * 1024 * 1024
LOG2E = math.log2(math.e)

C_CQ = 0
C_CKV = C_CQ + Q_LORA
C_M1 = C_CKV + KV_LORA
C_M2 = C_M1 + LANES
C_QKV = C_M2 + LANES
C_Z = C_QKV + 2 * DN_QK + DN_VW
C_GATE = C_Z + DN_VW
C_END = C_GATE + 2 * D_MODEL
C_CTX_END = C_Z


def _mm(a, b):
    return jnp.dot(a, b, preferred_element_type=F32)


def _mm_nt(a, b):
    return lax.dot_general(a, b, (((1,), (1,)), ((), ())), preferred_element_type=F32)


def _sigmoid(x):
    return 1.0 / (1.0 + jnp.exp(-x))


def _silu(x):
    return x * _sigmoid(x)


def _rms(x, g):
    return x * lax.rsqrt(jnp.mean(x * x, axis=-1, keepdims=True) + EPS) * g


def _params(n_axes):
    return pltpu.CompilerParams(dimension_semantics=("arbitrary",) * n_axes, vmem_limit_bytes=VMEM_LIMIT)


def _const_spec(shape):
    nd = len(shape)
    return pl.BlockSpec(shape, lambda *_: (0,) * nd, pipeline_mode=pl.Buffered(1))


def _mod_body(c_ref, w_ref, b_ref, o_ref):
    s = _silu(c_ref[...])
    o_ref[...] = jnp.dot(s, w_ref[...], preferred_element_type=F32, precision=lax.Precision.HIGHEST) + b_ref[...]


def _modulation(cc, w_mod, b_mod):
    rows, d = cc.shape
    n = w_mod.shape[1]
    tn = 1024
    return pl.pallas_call(
        _mod_body,
        grid=(n // tn,),
        in_specs=[pl.BlockSpec((rows, d), lambda j: (0, 0)),
                  pl.BlockSpec((d, tn), lambda j: (0, j)),
                  pl.BlockSpec((1, tn), lambda j: (0, j))],
        out_specs=pl.BlockSpec((rows, tn), lambda j: (0, j)),
        out_shape=jax.ShapeDtypeStruct((rows, n), F32),
        compiler_params=_params(1),
        name="modulation",
    )(cc, w_mod, b_mod.reshape(1, n))


def _inproj_body(x_ref, sh_ref, sc_ref, gpre_ref, w_ref, gq_ref, wq1_ref, wq2_ref, gkv_ref, wk_ref, wv_ref,
                 cq_tab, sq_tab, ck_tab, sk_tab, *outs, full):
    if full:
        q_ref, k_ref, v_ref, qkv_ref, ba_ref, z_ref, gate_ref = outs
    else:
        k_ref, v_ref, qkv_ref, ba_ref = outs
    u = (_rms(x_ref[...], gpre_ref[...]) * (1.0 + sc_ref[...]) + sh_ref[...]).astype(BF16)

    if full:
        nq = _rms(_mm(u, w_ref[:, C_CQ:C_CKV]), gq_ref[...]).astype(BF16)
        qa = _mm(nq, wq1_ref[...])
        qb = _mm(nq, wq2_ref[...])
        cq = cq_tab[...]
        sq = sq_tab[...]
        for h in range(MLA_HEADS):
            sl = slice(h * LANES, (h + 1) * LANES)
            q_ref[:, sl] = (qa[:, sl] * cq + qb[:, sl] * sq).astype(BF16)

    nkv = _rms(_mm(u, w_ref[:, C_CKV:C_M1]), gkv_ref[...]).astype(BF16)
    m1 = _mm(u, w_ref[:, C_M1:C_M2])
    m2 = _mm(u, w_ref[:, C_M2:C_QKV])
    ba_ref[...] = m1
    k_rope = m1 * ck_tab[...] + m2 * sk_tab[...]
    kn = _mm(nkv, wk_ref[...])
    for h in range(MLA_HEADS):
        sl = slice(h * LANES, (h + 1) * LANES)
        k_ref[:, sl] = (kn[:, sl] + k_rope).astype(BF16)
    v_ref[...] = _mm(nkv, wv_ref[...]).astype(BF16)
    qkv_ref[...] = _mm(u, w_ref[:, C_QKV:C_Z]).astype(BF16)
    if full:
        z_ref[...] = _mm(u, w_ref[:, C_Z:C_GATE]).astype(BF16)
        gate_ref[...] = _sigmoid(_mm(u, w_ref[:, C_GATE:C_END])).astype(BF16)


def _input_projection(x2, mod3, mod_period, tab_period, w, gpre, gq, wq1, wq2, gkv, wk, wv, tabs, full):
    n, d = x2.shape
    tm = min(ROW_TILE, tab_period)
    mod_tiles = mod_period // tm
    tab_tiles = tab_period // tm
    row = lambda i: (i, 0)
    tab = lambda i: (i % tab_tiles, 0)
    in_specs = [
        pl.BlockSpec((tm, d), row),
        pl.BlockSpec((None, 1, d), lambda i: (i // mod_tiles, 0, 0)),
        pl.BlockSpec((None, 1, d), lambda i: (i // mod_tiles, 0, 1)),
        _const_spec(gpre.shape), _const_spec(w.shape), _const_spec(gq.shape), _const_spec(wq1.shape),
        _const_spec(wq2.shape), _const_spec(gkv.shape), _const_spec(wk.shape), _const_spec(wv.shape),
        pl.BlockSpec((tm, LANES), tab), pl.BlockSpec((tm, LANES), tab), pl.BlockSpec((tm, LANES), tab),
        pl.BlockSpec((tm, LANES), tab),
    ]
    kw = MLA_HEADS * LANES
    vw = MLA_HEADS * MLA_V
    qkvw = 2 * DN_QK + DN_VW
    outs = [(kw, BF16), (vw, BF16), (qkvw, BF16), (LANES, F32)]
    if full:
        outs = [(kw, BF16)] + outs + [(DN_VW, BF16), (2 * D_MODEL, BF16)]
    return pl.pallas_call(
        functools.partial(_inproj_body, full=full),
        grid=(n // tm,),
        in_specs=in_specs,
        out_specs=[pl.BlockSpec((tm, c), row) for c, _ in outs],
        out_shape=[jax.ShapeDtypeStruct((n, c), t) for c, t in outs],
        compiler_params=_params(1),
        name="input_projection" if full else "input_projection_ctx",
    )(x2, mod3, mod3, gpre, w, gq, wq1, wq2, gkv, wk, wv, *tabs)


def _attn_body(q_ref, kl_ref, kc_ref, vl_ref, vc_ref, o_ref):
    for h in range(MLA_HEADS):
        sl = slice(h * LANES, (h + 1) * LANES)
        sv = slice(h * MLA_V, (h + 1) * MLA_V)
        qh = q_ref[:, sl]
        s1 = _mm_nt(qh, kl_ref[:, sl])
        s2 = _mm_nt(qh, kc_ref[:, sl])
        m = jnp.maximum(jnp.max(s1, axis=-1, keepdims=True), jnp.max(s2, axis=-1, keepdims=True))
        p1 = jnp.exp2(s1 - m)
        p2 = jnp.exp2(s2 - m)
        l = jnp.sum(p1, axis=-1, keepdims=True) + jnp.sum(p2, axis=-1, keepdims=True)
        g0 = h // V_GROUP * V_GROUP * MLA_V
        sg = slice(g0, g0 + V_GROUP * MLA_V)
        og = _mm(p1.astype(BF16), vl_ref[:, sg]) + _mm(p2.astype(BF16), vc_ref[:, sg])
        o = og[:, h % V_GROUP * MLA_V:(h % V_GROUP + 1) * MLA_V]
        o_ref[:, sv] = (o / l).astype(BF16)


def _attention(q, kl, kc, vl, vc):
    b, t, kw = q.shape
    s = kc.shape[1]
    vw = vl.shape[2]
    tq = min(Q_TILE, t)
    return pl.pallas_call(
        _attn_body,
        grid=(b, t // tq),
        in_specs=[pl.BlockSpec((None, tq, kw), lambda i, j: (i, j, 0)),
                  pl.BlockSpec((None, t, kw), lambda i, j: (i, 0, 0)),
                  pl.BlockSpec((None, s, kw), lambda i, j: (i, 0, 0)),
                  pl.BlockSpec((None, t, vw), lambda i, j: (i, 0, 0)),
                  pl.BlockSpec((None, s, vw), lambda i, j: (i, 0, 0))],
        out_specs=pl.BlockSpec((None, tq, vw), lambda i, j: (i, j, 0)),
        out_shape=jax.ShapeDtypeStruct((b, t, vw), BF16),
        compiler_params=_params(2),
        name="attention",
    )(q, kl, kc, vl, vc)


def _dn_prep_body(x_ref, xp_ref, xn_ref, ba_ref, cw_ref, alog_ref, dtb_ref, *outs, nchunks, emit):
    c = pl.program_id(1)
    C = DN_CHUNK
    nsub = x_ref.shape[0] // C
    has_prev = jnp.where(c > 0, 1.0, 0.0)
    has_next = jnp.where(c < nchunks // nsub - 1, 1.0, 0.0)
    rowi = lax.broadcasted_iota(jnp.int32, (C, C), 0)
    coli = lax.broadcasted_iota(jnp.int32, (C, C), 1)

    def l2n(a):
        return a * lax.rsqrt(jnp.sum(a * a, axis=-1, keepdims=True) + EPS)

    systems = []
    for j in range(nsub):
        _dn_prep_chunk(j, nsub, x_ref, xp_ref, xn_ref, ba_ref, cw_ref, alog_ref, dtb_ref, outs, emit, has_prev,
                       has_next, rowi, coli, l2n, systems)
    u_ref, w_ref = outs[0], outs[1]

    rs = [jnp.where((rowi >> 1) == (coli >> 1), sysm[3], 0.0) for sysm in systems]
    for lg in range(1, int(math.log2(C))):
        cross = ((rowi >> (lg + 1)) == (coli >> (lg + 1))) & ((rowi >> lg) != (coli >> lg))
        rbs = [r.astype(BF16) for r in rs]
        ms = []
        for sysm, rb in zip(systems, rbs):
            ns = jnp.where(cross, sysm[3], 0.0)
            ms.append(ns + _mm(rb, ns.astype(BF16)))
        rs = [r + m + _mm(m.astype(BF16), rb) for r, m, rb in zip(rs, ms, rbs)]
    for (j, d, hs, _, rhs), r in zip(systems, rs):
        rows = slice(j * C, (j + 1) * C)
        uw = rhs + _mm(r.astype(BF16), rhs.astype(BF16))
        u_ref[d, rows, hs] = uw[:, :LANES].astype(BF16)
        w_ref[d, rows, hs] = uw[:, LANES:].astype(BF16)


def _dn_prep_chunk(j, nsub, x_ref, xp_ref, xn_ref, ba_ref, cw_ref, alog_ref, dtb_ref, outs, emit, has_prev, has_next,
                   rowi, coli, l2n, systems):
    if emit:
        u_ref, w_ref, kdt_ref, dl_ref, qg_ref, qkm_ref = outs
    else:
        u_ref, w_ref, kdt_ref, dl_ref = outs
    C = DN_CHUNK
    rows = slice(j * C, (j + 1) * C)

    def conv_slab(s):
        sl = slice(s * LANES, (s + 1) * LANES)
        x = x_ref[rows, sl].astype(F32)
        if j == 0:
            prev = xp_ref[PACK_ROWS - 1:PACK_ROWS, sl].astype(F32) * has_prev
        else:
            prev = x_ref[j * C - 1:j * C, sl].astype(F32)
        if j == nsub - 1:
            nxt = xn_ref[0:1, sl].astype(F32) * has_next
        else:
            nxt = x_ref[(j + 1) * C:(j + 1) * C + 1, sl].astype(F32)
        xd = jnp.where(rowi == 0, prev, pltpu.roll(x, 1, 0))
        xu = jnp.where(rowi == C - 1, nxt, pltpu.roll(x, C - 1, 0))
        return _silu(xd * cw_ref[0:1, sl] + x * cw_ref[1:2, sl] + xu * cw_ref[2:3, sl])

    bg = ba_ref[rows, :]
    beta = _sigmoid(bg)
    xs = bg + dtb_ref[...]
    softplus = jnp.maximum(xs, 0.0) + jnp.log(1.0 + jnp.exp(-jnp.abs(xs)))
    g = -jnp.exp(alog_ref[...]) * softplus
    gc = g
    sft = 1
    while sft < C:
        gc = gc + jnp.where(rowi >= sft, pltpu.roll(gc, sft, 0), 0.0)
        sft *= 2
    gtot = gc[C - 1:C, :]
    gcr = gtot - gc + g
    gc_t = gc.T
    gcr_t = gcr.T
    e_gc = jnp.exp(gc)
    e_gcr = jnp.exp(gcr)
    e_rem = jnp.exp(gtot - gc)
    e_rem_r = jnp.exp(gtot - gcr)
    dl_all = jnp.exp(jnp.broadcast_to(gc_t[8:16, C - 1:C], (8, LANES)))
    scale = DN_DK ** -0.5

    for h in range(DN_HEADS):
        hs = slice(h * LANES, (h + 1) * LANES)
        qh = l2n(conv_slab(h)) * scale
        kh = l2n(conv_slab(DN_HEADS + h))
        vh = conv_slab(2 * DN_HEADS + h)
        khb = kh.astype(BF16)
        kk = _mm_nt(khb, khb)
        if emit:
            qk = _mm_nt(qh.astype(BF16), khb)
        for d in range(2):
            lane = 8 + 4 * d + h
            gsrc, gsrc_t, esrc, erem = (gc, gc_t, e_gc, e_rem) if d == 0 else (gcr, gcr_t, e_gcr, e_rem_r)
            gcol = gsrc[:, lane:lane + 1]
            grow = gsrc_t[lane:lane + 1, :]
            bcol = beta[:, 4 * d + h:4 * d + h + 1]
            incl = (rowi >= coli) if d == 0 else (rowi <= coli)
            strict = (rowi > coli) if d == 0 else (rowi < coli)
            decay = jnp.exp(jnp.where(incl, gcol - grow, -1e30))
            ecol = esrc[:, lane:lane + 1]
            systems.append((j, d, hs, jnp.where(strict, -(kk * bcol) * decay, 0.0),
                            jnp.concatenate([vh * bcol, kh * (bcol * ecol)], axis=1)))
            kdt_ref[d, rows, hs] = (kh * erem[:, lane:lane + 1]).T.astype(BF16)
            dl_ref[d, j, h:h + 1, :] = dl_all[4 * d + h:4 * d + h + 1, :]
            if emit:
                qg_ref[d, rows, hs] = (qh * ecol).astype(BF16)
                qkm_ref[d, rows, hs] = jnp.where(incl, qk * decay, 0.0).astype(BF16)


def _dn_prepare(qkv, ba, conv_w, alog, dtb, emit):
    b, t, cw = qkv.shape
    nchunks = t // DN_CHUNK
    C = PREP_CHUNKS * DN_CHUNK
    per = C // PACK_ROWS
    last = t // PACK_ROWS - 1
    seq_out = pl.BlockSpec((None, 2, C, DN_VW), lambda i, c: (i, 0, c, 0))
    seq_shape = jax.ShapeDtypeStruct((b, 2, t, DN_VW), BF16)
    out_specs = [seq_out, seq_out, seq_out,
                 pl.BlockSpec((None, 2, PREP_CHUNKS, DN_HEADS, LANES), lambda i, c: (i, 0, c, 0, 0))]
    out_shape = [seq_shape, seq_shape, seq_shape, jax.ShapeDtypeStruct((b, 2, nchunks, DN_HEADS, LANES), F32)]
    if emit:
        out_specs += [seq_out, seq_out]
        out_shape += [seq_shape, seq_shape]
    return pl.pallas_call(
        functools.partial(_dn_prep_body, nchunks=nchunks, emit=emit),
        grid=(b, t // C),
        in_specs=[pl.BlockSpec((None, C, cw), lambda i, c: (i, c, 0)),
                  pl.BlockSpec((None, PACK_ROWS, cw), lambda i, c: (i, jnp.maximum(c * per - 1, 0), 0)),
                  pl.BlockSpec((None, PACK_ROWS, cw), lambda i, c: (i, jnp.minimum((c + 1) * per, last), 0)),
                  pl.BlockSpec((None, C, LANES), lambda i, c: (i, c, 0)),
                  _const_spec(conv_w.shape), _const_spec(alog.shape), _const_spec(dtb.shape)],
        out_specs=out_specs,
        out_shape=out_shape,
        compiler_params=_params(2),
        name="deltanet_prepare" if emit else "deltanet_prepare_ctx",
    )(qkv, qkv, qkv, ba, conv_w, alog, dtb)


def _dn_scan_body(*refs, n_ctx, per_block):
    ctx_f, ctx_b = refs[0:4], refs[4:8]
    lat_f, lat_b = refs[8:14], refs[14:20]
    of_ref, ob_ref, s_ref = refs[20:23]
    C = DN_CHUNK
    heads = range(DN_HEADS)
    hs = [slice(h * LANES, (h + 1) * LANES) for h in heads]

    def step(srcs):
        chains = [(d, lc, ops, o_ref, h) for d, lc, ops, o_ref in srcs for h in heads]
        rows = lambda lc: slice(lc * C, (lc + 1) * C)
        ss = [s_ref[d, h] for d, _, _, _, h in chains]
        sbs = [s.astype(BF16) for s in ss]
        ws = [_mm(ops[1][rows(lc), hs[h]], sb) for (_, lc, ops, _, h), sb in zip(chains, sbs)]
        os = [_mm(ops[4][rows(lc), hs[h]], sb) if o_ref is not None else None
              for (_, lc, ops, o_ref, h), sb in zip(chains, sbs)]
        vnbs = [(ops[0][rows(lc), hs[h]].astype(F32) - w).astype(BF16) for (_, lc, ops, _, h), w in zip(chains, ws)]
        for (d, lc, ops, _, h), s, vnb in zip(chains, ss, vnbs):
            s_ref[d, h] = s * ops[3][lc, h:h + 1, :] + _mm(ops[2][rows(lc), hs[h]], vnb)
        for (_, lc, ops, o_ref, h), o, vnb in zip(chains, os, vnbs):
            if o_ref is not None:
                o_ref[rows(lc), hs[h]] = (o + _mm(ops[5][rows(lc), hs[h]], vnb)).astype(BF16)

    @pl.when(pl.program_id(1) == 0)
    def _():
        s_ref[...] = jnp.zeros_like(s_ref)
        for c in range(n_ctx):
            step([(0, c, ctx_f, None), (1, n_ctx - 1 - c, ctx_b, None)])

    for c in range(per_block):
        step([(0, c, lat_f, of_ref), (1, per_block - 1 - c, lat_b, ob_ref)])


def _dn_scan(ctx_ops, lat_ops):
    b, _, t, vw = lat_ops[0].shape
    s = ctx_ops[0].shape[2]
    sb = min(SCAN_BLOCK, t)
    per_block, nblk, n_ctx = sb // DN_CHUNK, t // sb, s // DN_CHUNK
    blk = lambda d: (lambda i, j: (i, d, j if d == 0 else nblk - 1 - j, 0))
    seq_c = lambda d: pl.BlockSpec((None, None, s, vw), lambda i, j: (i, d, 0, 0))
    dl_c = lambda d: pl.BlockSpec((None, None, n_ctx, DN_HEADS, LANES), lambda i, j: (i, d, 0, 0, 0))
    seq_l = lambda d: pl.BlockSpec((None, None, sb, vw), blk(d))
    dl_l = lambda d: pl.BlockSpec((None, None, per_block, DN_HEADS, LANES), lambda i, j: blk(d)(i, j) + (0,))
    ctx_specs = lambda d: [seq_c(d), seq_c(d), seq_c(d), dl_c(d)]
    lat_specs = lambda d: [seq_l(d), seq_l(d), seq_l(d), dl_l(d), seq_l(d), seq_l(d)]
    out_f = pl.BlockSpec((None, sb, vw), lambda i, j: (i, j, 0))
    out_b = pl.BlockSpec((None, sb, vw), lambda i, j: (i, nblk - 1 - j, 0))
    out_shape = jax.ShapeDtypeStruct((b, t, vw), BF16)
    return pl.pallas_call(
        functools.partial(_dn_scan_body, n_ctx=n_ctx, per_block=per_block),
        grid=(b, nblk),
        in_specs=ctx_specs(0) + ctx_specs(1) + lat_specs(0) + lat_specs(1),
        out_specs=[out_f, out_b],
        out_shape=[out_shape, out_shape],
        scratch_shapes=[pltpu.VMEM((2, DN_HEADS, DN_DK, DN_DV), F32)],
        compiler_params=_params(2),
        name="deltanet_scan",
    )(*ctx_ops, *ctx_ops, *lat_ops, *lat_ops)


def _merge_ffn_body(x_ref, om_ref, of_ref, ob_ref, z_ref, gate_ref, gt1_ref, sh2_ref, sc2_ref, gt2_ref,
                    gdn_ref, woa_ref, wod_ref, wout_ref, gpost_ref, gpre2_ref, w1_ref, w2_ref, gpost2_ref, out_ref):
    odn = of_ref[...].astype(F32) + ob_ref[...].astype(F32)
    parts = []
    for h in range(DN_HEADS):
        hs = slice(h * DN_DV, (h + 1) * DN_DV)
        parts.append(_rms(odn[:, hs], gdn_ref[...]) * _silu(z_ref[:, hs].astype(F32)))
    o_dn = jnp.concatenate(parts, axis=1).astype(BF16)
    ya = _mm(om_ref[...], woa_ref[...])
    yb = _mm(o_dn, wod_ref[...])
    mix = gate_ref[:, :D_MODEL].astype(F32) * ya + gate_ref[:, D_MODEL:].astype(F32) * yb
    y = _mm(mix.astype(BF16), wout_ref[...])
    x1 = x_ref[...] + gt1_ref[...] * _rms(y, gpost_ref[...])
    u2 = (_rms(x1, gpre2_ref[...]) * (1.0 + sc2_ref[...]) + sh2_ref[...]).astype(BF16)
    acc = jnp.zeros(x1.shape, F32)
    for j in range(FF_HIDDEN // FF_TILE):
        a = _mm(u2, w1_ref[:, j * FF_TILE:(j + 1) * FF_TILE])
        up = _mm(u2, w1_ref[:, FF_HIDDEN + j * FF_TILE:FF_HIDDEN + (j + 1) * FF_TILE])
        acc = acc + _mm((_silu(a) * up).astype(BF16), w2_ref[j * FF_TILE:(j + 1) * FF_TILE, :])
    out_ref[...] = x1 + gt2_ref[...] * _rms(acc, gpost2_ref[...])


def _merge_ffn(x2, seq, om, o_f, o_b, z, gates, mod3, gdn, woa, wod, wout, gpost, gpre2, w1, w2, gpost2):
    n, d = x2.shape
    tm = min(ROW_TILE, seq)
    tiles = seq // tm
    row = lambda i: (i, 0)
    modspec = lambda k: pl.BlockSpec((None, 1, d), lambda i: (i // tiles, 0, k))
    dn_spec = pl.BlockSpec((tm, DN_VW), row)
    return pl.pallas_call(
        _merge_ffn_body,
        grid=(n // tm,),
        in_specs=[pl.BlockSpec((tm, d), row), pl.BlockSpec((tm, om.shape[1]), row), dn_spec, dn_spec,
                  pl.BlockSpec((tm, DN_VW), row), pl.BlockSpec((tm, 2 * d), row),
                  modspec(2), modspec(3), modspec(4), modspec(5),
                  _const_spec(gdn.shape), _const_spec(woa.shape), _const_spec(wod.shape), _const_spec(wout.shape),
                  _const_spec(gpost.shape), _const_spec(gpre2.shape), _const_spec(w1.shape), _const_spec(w2.shape),
                  _const_spec(gpost2.shape)],
        out_specs=pl.BlockSpec((tm, d), row),
        out_shape=jax.ShapeDtypeStruct((n, d), F32),
        compiler_params=_params(1),
        name="merge_ffn",
    )(x2, om, o_f, o_b, z, gates, mod3, mod3, mod3, mod3, gdn, woa, wod, wout, gpost, gpre2, w1, w2, gpost2)


def _rot_half(w):
    n = MLA_ROPE // 4
    return jnp.concatenate([-w[..., n:2 * n], w[..., 0:n], -w[..., 3 * n:4 * n], w[..., 2 * n:3 * n]], axis=-1)


def _rope_tables(rows):
    row = jnp.repeat(jnp.arange(rows, dtype=F32), GRID_W)
    col = jnp.tile(jnp.arange(GRID_W, dtype=F32), rows)
    n = MLA_ROPE // 4
    inv = ROPE_BASE ** (-jnp.arange(n, dtype=F32) / n)
    ar, ac = row[:, None] * inv, col[:, None] * inv
    cos = jnp.concatenate([jnp.cos(ar), jnp.cos(ar), jnp.cos(ac), jnp.cos(ac)], axis=-1)
    sin = jnp.concatenate([jnp.sin(ar), jnp.sin(ar), jnp.sin(ac), jnp.sin(ac)], axis=-1)
    return cos, sin


def _pad_tab(nope_val, rope, t):
    return jnp.concatenate([jnp.full((t, MLA_NOPE), nope_val, F32), rope,
                            jnp.zeros((t, LANES - MLA_NOPE - MLA_ROPE), F32)], axis=-1)


def kernel(x, c, ctx, c_ctx, w_mod, b_mod, g_pre_mix, g_post_mix, g_pre_ffn, g_post_ffn, w_in, g_q_lora, w_uq,
           g_kv_lora, w_ukv, w_o_mla, conv_qkv, a_log, dt_bias, g_dn_out, w_o_dn, w_out, w_ffn_in, w_ffn_out):
    B, T, D = x.shape
    S = ctx.shape[1]
    assert w_mod.shape[0] == 1 and D == D_MODEL
    assert T % max(ROW_TILE, Q_TILE, SCAN_BLOCK) == 0 and S % (PREP_CHUNKS * DN_CHUNK) == 0 and T % GRID_W == 0
    L = 0
    row2 = lambda a: a[L].reshape(1, -1)

    mod_rows = -(-(B + 1) // 8) * 8
    cc = jnp.concatenate([c, c_ctx[None, :], jnp.zeros((mod_rows - B - 1, D), F32)], axis=0)
    mod3 = _modulation(cc, w_mod[L], b_mod[L]).reshape(mod_rows, 1, 6 * D)

    wi = w_in[L]
    o_cq, o_ckv, o_kr = 0, Q_LORA, Q_LORA + KV_LORA
    o_qkv = o_kr + MLA_ROPE
    o_z = o_qkv + 2 * DN_QK + DN_VW
    o_beta = o_z + DN_VW
    o_gate = o_beta + 4 * DN_HEADS
    w_kr = wi[:, o_kr:o_qkv]
    zc = lambda n: jnp.zeros((D, n), F32)
    w_pack = jnp.concatenate([
        wi[:, o_cq:o_kr],
        wi[:, o_beta:o_gate], zc(MLA_NOPE - 4 * DN_HEADS), w_kr, zc(LANES - MLA_NOPE - MLA_ROPE),
        zc(MLA_NOPE), _rot_half(w_kr), zc(LANES - MLA_NOPE - MLA_ROPE),
        wi[:, o_qkv:o_beta], wi[:, o_gate:]], axis=1).astype(BF16)
    assert w_pack.shape[1] == C_END

    dq = MLA_NOPE + MLA_ROPE
    wq = w_uq[L].reshape(Q_LORA, MLA_HEADS, dq)
    zq = jnp.zeros((Q_LORA, MLA_HEADS, LANES - dq), F32)
    wq1 = jnp.concatenate([wq, zq], axis=-1).reshape(Q_LORA, MLA_HEADS * LANES).astype(BF16)
    wq2 = jnp.concatenate([jnp.zeros((Q_LORA, MLA_HEADS, MLA_NOPE), F32), _rot_half(wq[..., MLA_NOPE:]), zq],
                          axis=-1).reshape(Q_LORA, MLA_HEADS * LANES).astype(BF16)
    wkv = w_ukv[L].reshape(KV_LORA, MLA_HEADS, MLA_NOPE + MLA_V)
    wk = jnp.concatenate([wkv[..., :MLA_NOPE], jnp.zeros((KV_LORA, MLA_HEADS, LANES - MLA_NOPE), F32)],
                         axis=-1).reshape(KV_LORA, MLA_HEADS * LANES).astype(BF16)
    wv = wkv[..., MLA_NOPE:].reshape(KV_LORA, MLA_HEADS * MLA_V).astype(BF16)

    cos, sin = _rope_tables(T // GRID_W)
    qs = dq ** -0.5 * LOG2E
    tabs_lat = (_pad_tab(qs, cos * qs, T), _pad_tab(0.0, sin * qs, T), _pad_tab(0.0, cos, T), _pad_tab(0.0, sin, T))
    ones = jnp.ones((S, MLA_ROPE), F32)
    no_rot = _pad_tab(0.0, 0.0 * ones, S)
    tabs_ctx = (no_rot, no_rot, _pad_tab(0.0, ones, S), no_rot)

    x2 = x.reshape(B * T, D)
    c2 = ctx.reshape(B * S, D)
    gpre, gq, gkv = row2(g_pre_mix), row2(g_q_lora), row2(g_kv_lora)
    q, k, v, qkv, ba, z, gates = _input_projection(
        x2, mod3, T, T, w_pack, gpre, gq, wq1, wq2, gkv, wk, wv, tabs_lat, True)
    k_c, v_c, qkv_c, ba_c = _input_projection(
        c2, mod3[B:B + 1], B * S, S, w_pack[:, :C_CTX_END], gpre, gq, wq1, wq2, gkv, wk, wv, tabs_ctx, False)

    r3 = lambda a, n: a.reshape(B, n, a.shape[-1])
    o_mla = _attention(r3(q, T), r3(k, T), r3(k_c, S), r3(v, T), r3(v_c, S))

    lane_pad = lambda a, fill: jnp.pad(a.astype(F32).reshape(1, -1), ((0, 0), (8, LANES - 16)), constant_values=fill)
    alog = lane_pad(a_log[L], -1e30)
    dtb = lane_pad(dt_bias[L], 0.0)
    cw = conv_qkv[L].astype(F32)
    ctx_ops = _dn_prepare(r3(qkv_c, S), r3(ba_c, S), cw, alog, dtb, False)
    lat_ops = _dn_prepare(r3(qkv, T), r3(ba, T), cw, alog, dtb, True)
    o_f, o_b = _dn_scan(ctx_ops, lat_ops)

    out = _merge_ffn(x2, T, o_mla.reshape(B * T, -1), o_f.reshape(B * T, -1), o_b.reshape(B * T, -1), z, gates, mod3,
                     row2(g_dn_out),
                     w_o_mla[L].astype(BF16), w_o_dn[L].astype(BF16), w_out[L].astype(BF16), row2(g_post_mix),
                     row2(g_pre_ffn), w_ffn_in[L].astype(BF16), w_ffn_out[L].astype(BF16), row2(g_post_ffn))
    return out.reshape(B, T, D)
```

```python
import functools
import math

import jax
import jax.numpy as jnp
from jax import lax
from jax.experimental import pallas as pl
from jax.experimental.pallas import tpu as pltpu

F32 = jnp.float32
BF16 = jnp.bfloat16

D_MODEL = 1024
GRID_W = 64
MLA_HEADS = 8
MLA_NOPE = 64
MLA_ROPE = 32
MLA_V = 64
Q_LORA = 512
KV_LORA = 256
ROPE_BASE = 10000.0
DN_HEADS = 4
DN_DK = 128
DN_DV = 128
DN_QK = DN_HEADS * DN_DK
DN_VW = DN_HEADS * DN_DV
FF_HIDDEN = 2816
EPS = 1e-6

LANES = 128
DN_CHUNK = 128
ROW_TILE = 512
Q_TILE = 512
V_GROUP = 4
SCAN_BLOCK = 512
PREP_CHUNKS = 2
FF_TILE = 256
PACK_ROWS = 16
VMEM_LIMIT = 56 * 1024 * 1024
LOG2E = math.log2(math.e)

C_CQ = 0
C_CKV = C_CQ + Q_LORA
C_M1 = C_CKV + KV_LORA
C_M2 = C_M1 + LANES
C_QKV = C_M2 + LANES
C_Z = C_QKV + 2 * DN_QK + DN_VW
C_GATE = C_Z + DN_VW
C_END = C_GATE + 2 * D_MODEL
C_CTX_END = C_Z


def _mm(a, b):
    return jnp.dot(a, b, preferred_element_type=F32)


def _mm_nt(a, b):
    return lax.dot_general(a, b, (((1,), (1,)), ((), ())), preferred_element_type=F32)


def _sigmoid(x):
    return 1.0 / (1.0 + jnp.exp(-x))


def _silu(x):
    return x * _sigmoid(x)


def _rms(x, g):
    return x * lax.rsqrt(jnp.mean(x * x, axis=-1, keepdims=True) + EPS) * g


def _params(n_axes):
    return pltpu.CompilerParams(dimension_semantics=("arbitrary",) * n_axes, vmem_limit_bytes=VMEM_LIMIT)


def _const_spec(shape):
    nd = len(shape)
    return pl.BlockSpec(shape, lambda *_: (0,) * nd, pipeline_mode=pl.Buffered(1))


def _mod_body(c_ref, w_ref, b_ref, o_ref):
    s = _silu(c_ref[...])
    o_ref[...] = jnp.dot(s, w_ref[...], preferred_element_type=F32, precision=lax.Precision.HIGHEST) + b_ref[...]


def _modulation(cc, w_mod, b_mod):
    rows, d = cc.shape
    n = w_mod.shape[1]
    tn = 1024
    return pl.pallas_call(
        _mod_body,
        grid=(n // tn,),
        in_specs=[pl.BlockSpec((rows, d), lambda j: (0, 0)),
                  pl.BlockSpec((d, tn), lambda j: (0, j)),
                  pl.BlockSpec((1, tn), lambda j: (0, j))],
        out_specs=pl.BlockSpec((rows, tn), lambda j: (0, j)),
        out_shape=jax.ShapeDtypeStruct((rows, n), F32),
        compiler_params=_params(1),
        name="modulation",
    )(cc, w_mod, b_mod.reshape(1, n))


def _inproj_body(x_ref, sh_ref, sc_ref, gpre_ref, w_ref, gq_ref, wq1_ref, wq2_ref, gkv_ref, wk_ref, wv_ref,
                 cq_tab, sq_tab, ck_tab, sk_tab, *outs, full):
    if full:
        q_ref, k_ref, v_ref, qkv_ref, ba_ref, z_ref, gate_ref = outs
    else:
        k_ref, v_ref, qkv_ref, ba_ref = outs
    u = (_rms(x_ref[...], gpre_ref[...]) * (1.0 + sc_ref[...]) + sh_ref[...]).astype(BF16)

    if full:
        nq = _rms(_mm(u, w_ref[:, C_CQ:C_CKV]), gq_ref[...]).astype(BF16)
        qa = _mm(nq, wq1_ref[...])
        qb = _mm(nq, wq2_ref[...])
        cq = cq_tab[...]
        sq = sq_tab[...]
        for h in range(MLA_HEADS):
            sl = slice(h * LANES, (h + 1) * LANES)
            q_ref[:, sl] = (qa[:, sl] * cq + qb[:, sl] * sq).astype(BF16)

    nkv = _rms(_mm(u, w_ref[:, C_CKV:C_M1]), gkv_ref[...]).astype(BF16)
    m1 = _mm(u, w_ref[:, C_M1:C_M2])
    m2 = _mm(u, w_ref[:, C_M2:C_QKV])
    ba_ref[...] = m1
    k_rope = m1 * ck_tab[...] + m2 * sk_tab[...]
    kn = _mm(nkv, wk_ref[...])
    for h in range(MLA_HEADS):
        sl = slice(h * LANES, (h + 1) * LANES)
        k_ref[:, sl] = (kn[:, sl] + k_rope).astype(BF16)
    v_ref[...] = _mm(nkv, wv_ref[...]).astype(BF16)
    qkv_ref[...] = _mm(u, w_ref[:, C_QKV:C_Z]).astype(BF16)
    if full:
        z_ref[...] = _mm(u, w_ref[:, C_Z:C_GATE]).astype(BF16)
        gate_ref[...] = _sigmoid(_mm(u, w_ref[:, C_GATE:C_END])).astype(BF16)


def _input_projection(x2, mod3, mod_period, tab_period, w, gpre, gq, wq1, wq2, gkv, wk, wv, tabs, full):
    n, d = x2.shape
    tm = min(ROW_TILE, tab_period)
    mod_tiles = mod_period // tm
    tab_tiles = tab_period // tm
    row = lambda i: (i, 0)
    tab = lambda i: (i % tab_tiles, 0)
    in_specs = [
        pl.BlockSpec((tm, d), row),
        pl.BlockSpec((None, 1, d), lambda i: (i // mod_tiles, 0, 0)),
        pl.BlockSpec((None, 1, d), lambda i: (i // mod_tiles, 0, 1)),
        _const_spec(gpre.shape), _const_spec(w.shape), _const_spec(gq.shape), _const_spec(wq1.shape),
        _const_spec(wq2.shape), _const_spec(gkv.shape), _const_spec(wk.shape), _const_spec(wv.shape),
        pl.BlockSpec((tm, LANES), tab), pl.BlockSpec((tm, LANES), tab), pl.BlockSpec((tm, LANES), tab),
        pl.BlockSpec((tm, LANES), tab),
    ]
    kw = MLA_HEADS * LANES
    vw = MLA_HEADS * MLA_V
    qkvw = 2 * DN_QK + DN_VW
    outs = [(kw, BF16), (vw, BF16), (qkvw, BF16), (LANES, F32)]
    if full:
        outs = [(kw, BF16)] + outs + [(DN_VW, BF16), (2 * D_MODEL, BF16)]
    return pl.pallas_call(
        functools.partial(_inproj_body, full=full),
        grid=(n // tm,),
        in_specs=in_specs,
        out_specs=[pl.BlockSpec((tm, c), row) for c, _ in outs],
        out_shape=[jax.ShapeDtypeStruct((n, c), t) for c, t in outs],
        compiler_params=_params(1),
        name="input_projection" if full else "input_projection_ctx",
    )(x2, mod3, mod3, gpre, w, gq, wq1, wq2, gkv, wk, wv, *tabs)


def _attn_body(q_ref, kl_ref, kc_ref, vl_ref, vc_ref, o_ref):
    for h in range(MLA_HEADS):
        sl = slice(h * LANES, (h + 1) * LANES)
        sv = slice(h * MLA_V, (h + 1) * MLA_V)
        qh = q_ref[:, sl]
        s1 = _mm_nt(qh, kl_ref[:, sl])
        s2 = _mm_nt(qh, kc_ref[:, sl])
        m = jnp.maximum(jnp.max(s1, axis=-1, keepdims=True), jnp.max(s2, axis=-1, keepdims=True))
        p1 = jnp.exp2(s1 - m)
        p2 = jnp.exp2(s2 - m)
        l = jnp.sum(p1, axis=-1, keepdims=True) + jnp.sum(p2, axis=-1, keepdims=True)
        g0 = h // V_GROUP * V_GROUP * MLA_V
        sg = slice(g0, g0 + V_GROUP * MLA_V)
        og = _mm(p1.astype(BF16), vl_ref[:, sg]) + _mm(p2.astype(BF16), vc_ref[:, sg])
        o = og[:, h % V_GROUP * MLA_V:(h % V_GROUP + 1) * MLA_V]
        o_ref[:, sv] = (o / l).astype(BF16)


def _attention(q, kl, kc, vl, vc):
    b, t, kw = q.shape
    s = kc.shape[1]
    vw = vl.shape[2]
    tq = min(Q_TILE, t)
    return pl.pallas_call(
        _attn_body,
        grid=(b, t // tq),
        in_specs=[pl.BlockSpec((None, tq, kw), lambda i, j: (i, j, 0)),
                  pl.BlockSpec((None, t, kw), lambda i, j: (i, 0, 0)),
                  pl.BlockSpec((None, s, kw), lambda i, j: (i, 0, 0)),
                  pl.BlockSpec((None, t, vw), lambda i, j: (i, 0, 0)),
                  pl.BlockSpec((None, s, vw), lambda i, j: (i, 0, 0))],
        out_specs=pl.BlockSpec((None, tq, vw), lambda i, j: (i, j, 0)),
        out_shape=jax.ShapeDtypeStruct((b, t, vw), BF16),
        compiler_params=_params(2),
        name="attention",
    )(q, kl, kc, vl, vc)


def _dn_prep_body(x_ref, xp_ref, xn_ref, ba_ref, cw_ref, alog_ref, dtb_ref, *outs, nchunks, emit):
    c = pl.program_id(1)
    C = DN_CHUNK
    nsub = x_ref.shape[0] // C
    has_prev = jnp.where(c > 0, 1.0, 0.0)
    has_next = jnp.where(c < nchunks // nsub - 1, 1.0, 0.0)
    rowi = lax.broadcasted_iota(jnp.int32, (C, C), 0)
    coli = lax.broadcasted_iota(jnp.int32, (C, C), 1)

    def l2n(a):
        return a * lax.rsqrt(jnp.sum(a * a, axis=-1, keepdims=True) + EPS)

    systems = []
    for j in range(nsub):
        _dn_prep_chunk(j, nsub, x_ref, xp_ref, xn_ref, ba_ref, cw_ref, alog_ref, dtb_ref, outs, emit, has_prev,
                       has_next, rowi, coli, l2n, systems)
    u_ref, w_ref = outs[0], outs[1]

    eye = jnp.where(rowi == coli, 1.0, 0.0)
    ts = [eye + jnp.where((rowi >> 1) == (coli >> 1), sysm[3], 0.0) for sysm in systems]
    nbs = [sysm[3].astype(BF16) for sysm in systems]
    zero = jnp.zeros((C, C), BF16)
    for lg in range(1, int(math.log2(C))):
        cross = ((rowi >> (lg + 1)) == (coli >> (lg + 1))) & ((rowi >> lg) != (coli >> lg))
        tbs = [t.astype(BF16) for t in ts]
        ms = [_mm(tb, jnp.where(cross, nb, zero)) for tb, nb in zip(tbs, nbs)]
        ts = [t + _mm(m.astype(BF16), tb) for t, m, tb in zip(ts, ms, tbs)]
    for (j, d, hs, _, rhs), t in zip(systems, ts):
        rows = slice(j * C, (j + 1) * C)
        uw = _mm(t.astype(BF16), rhs.astype(BF16))
        u_ref[d, rows, hs] = uw[:, :LANES].astype(BF16)
        w_ref[d, rows, hs] = uw[:, LANES:].astype(BF16)


def _dn_prep_chunk(j, nsub, x_ref, xp_ref, xn_ref, ba_ref, cw_ref, alog_ref, dtb_ref, outs, emit, has_prev, has_next,
                   rowi, coli, l2n, systems):
    if emit:
        u_ref, w_ref, kdt_ref, dl_ref, qg_ref, qkm_ref = outs
    else:
        u_ref, w_ref, kdt_ref, dl_ref = outs
    C = DN_CHUNK
    rows = slice(j * C, (j + 1) * C)

    def conv_slab(s):
        sl = slice(s * LANES, (s + 1) * LANES)
        x = x_ref[rows, sl].astype(F32)
        if j == 0:
            prev = xp_ref[PACK_ROWS - 1:PACK_ROWS, sl].astype(F32) * has_prev
        else:
            prev = x_ref[j * C - 1:j * C, sl].astype(F32)
        if j == nsub - 1:
            nxt = xn_ref[0:1, sl].astype(F32) * has_next
        else:
            nxt = x_ref[(j + 1) * C:(j + 1) * C + 1, sl].astype(F32)
        xd = jnp.where(rowi == 0, prev, pltpu.roll(x, 1, 0))
        xu = jnp.where(rowi == C - 1, nxt, pltpu.roll(x, C - 1, 0))
        return _silu(xd * cw_ref[0:1, sl] + x * cw_ref[1:2, sl] + xu * cw_ref[2:3, sl])

    bg = ba_ref[rows, :]
    beta = _sigmoid(bg)
    xs = bg + dtb_ref[...]
    softplus = jnp.maximum(xs, 0.0) + jnp.log(1.0 + jnp.exp(-jnp.abs(xs)))
    g = -jnp.exp(alog_ref[...]) * softplus
    gc = g
    sft = 1
    while sft < C:
        gc = gc + jnp.where(rowi >= sft, pltpu.roll(gc, sft, 0), 0.0)
        sft *= 2
    gtot = gc[C - 1:C, :]
    gcr = gtot - gc + g
    gc_t = gc.T
    gcr_t = gcr.T
    e_gc = jnp.exp(gc)
    e_gcr = jnp.exp(gcr)
    e_rem = jnp.exp(gtot - gc)
    e_rem_r = jnp.exp(gtot - gcr)
    dl_all = jnp.exp(jnp.broadcast_to(gc_t[8:16, C - 1:C], (8, LANES)))
    scale = DN_DK ** -0.5

    for h in range(DN_HEADS):
        hs = slice(h * LANES, (h + 1) * LANES)
        qh = l2n(conv_slab(h)) * scale
        kh = l2n(conv_slab(DN_HEADS + h))
        vh = conv_slab(2 * DN_HEADS + h)
        khb = kh.astype(BF16)
        kk = _mm_nt(khb, khb)
        if emit:
            qk = _mm_nt(qh.astype(BF16), khb)
        for d in range(2):
            lane = 8 + 4 * d + h
            gsrc, gsrc_t, esrc, erem = (gc, gc_t, e_gc, e_rem) if d == 0 else (gcr, gcr_t, e_gcr, e_rem_r)
            gcol = gsrc[:, lane:lane + 1]
            grow = gsrc_t[lane:lane + 1, :]
            bcol = beta[:, 4 * d + h:4 * d + h + 1]
            incl = (rowi >= coli) if d == 0 else (rowi <= coli)
            strict = (rowi > coli) if d == 0 else (rowi < coli)
            decay = jnp.exp(jnp.where(incl, gcol - grow, -1e30))
            ecol = esrc[:, lane:lane + 1]
            systems.append((j, d, hs, jnp.where(strict, -(kk * bcol) * decay, 0.0),
                            jnp.concatenate([vh * bcol, kh * (bcol * ecol)], axis=1)))
            kdt_ref[d, rows, hs] = (kh * erem[:, lane:lane + 1]).T.astype(BF16)
            dl_ref[d, j, h:h + 1, :] = dl_all[4 * d + h:4 * d + h + 1, :]
            if emit:
                qg_ref[d, rows, hs] = (qh * ecol).astype(BF16)
                qkm_ref[d, rows, hs] = jnp.where(incl, qk * decay, 0.0).astype(BF16)


def _dn_prepare(qkv, ba, conv_w, alog, dtb, emit):
    b, t, cw = qkv.shape
    nchunks = t // DN_CHUNK
    C = PREP_CHUNKS * DN_CHUNK
    per = C // PACK_ROWS
    last = t // PACK_ROWS - 1
    seq_out = pl.BlockSpec((None, 2, C, DN_VW), lambda i, c: (i, 0, c, 0))
    seq_shape = jax.ShapeDtypeStruct((b, 2, t, DN_VW), BF16)
    out_specs = [seq_out, seq_out, seq_out,
                 pl.BlockSpec((None, 2, PREP_CHUNKS, DN_HEADS, LANES), lambda i, c: (i, 0, c, 0, 0))]
    out_shape = [seq_shape, seq_shape, seq_shape, jax.ShapeDtypeStruct((b, 2, nchunks, DN_HEADS, LANES), F32)]
    if emit:
        out_specs += [seq_out, seq_out]
        out_shape += [seq_shape, seq_shape]
    return pl.pallas_call(
        functools.partial(_dn_prep_body, nchunks=nchunks, emit=emit),
        grid=(b, t // C),
        in_specs=[pl.BlockSpec((None, C, cw), lambda i, c: (i, c, 0)),
                  pl.BlockSpec((None, PACK_ROWS, cw), lambda i, c: (i, jnp.maximum(c * per - 1, 0), 0)),
                  pl.BlockSpec((None, PACK_ROWS, cw), lambda i, c: (i, jnp.minimum((c + 1) * per, last), 0)),
                  pl.BlockSpec((None, C, LANES), lambda i, c: (i, c, 0)),
                  _const_spec(conv_w.shape), _const_spec(alog.shape), _const_spec(dtb.shape)],
        out_specs=out_specs,
        out_shape=out_shape,
        compiler_params=_params(2),
        name="deltanet_prepare" if emit else "deltanet_prepare_ctx",
    )(qkv, qkv, qkv, ba, conv_w, alog, dtb)


def _dn_scan_body(*refs, n_ctx, per_block):
    ctx_f, ctx_b = refs[0:4], refs[4:8]
    lat_f, lat_b = refs[8:14], refs[14:20]
    of_ref, ob_ref, s_ref = refs[20:23]
    C = DN_CHUNK
    heads = range(DN_HEADS)
    hs = [slice(h * LANES, (h + 1) * LANES) for h in heads]

    def step(srcs):
        chains = [(d, lc, ops, o_ref, h) for d, lc, ops, o_ref in srcs for h in heads]
        rows = lambda lc: slice(lc * C, (lc + 1) * C)
        ss = [s_ref[d, h] for d, _, _, _, h in chains]
        sbs = [s.astype(BF16) for s in ss]
        ws = [_mm(ops[1][rows(lc), hs[h]], sb) for (_, lc, ops, _, h), sb in zip(chains, sbs)]
        os = [_mm(ops[4][rows(lc), hs[h]], sb) if o_ref is not None else None
              for (_, lc, ops, o_ref, h), sb in zip(chains, sbs)]
        vnbs = [(ops[0][rows(lc), hs[h]].astype(F32) - w).astype(BF16) for (_, lc, ops, _, h), w in zip(chains, ws)]
        for (d, lc, ops, _, h), s, vnb in zip(chains, ss, vnbs):
            s_ref[d, h] = s * ops[3][lc, h:h + 1, :] + _mm(ops[2][rows(lc), hs[h]], vnb)
        for (_, lc, ops, o_ref, h), o, vnb in zip(chains, os, vnbs):
            if o_ref is not None:
                o_ref[rows(lc), hs[h]] = (o + _mm(ops[5][rows(lc), hs[h]], vnb)).astype(BF16)

    @pl.when(pl.program_id(1) == 0)
    def _():
        s_ref[...] = jnp.zeros_like(s_ref)
        for c in range(n_ctx):
            step([(0, c, ctx_f, None), (1, n_ctx - 1 - c, ctx_b, None)])

    for c in range(per_block):
        step([(0, c, lat_f, of_ref), (1, per_block - 1 - c, lat_b, ob_ref)])


def _dn_scan(ctx_ops, lat_ops):
    b, _, t, vw = lat_ops[0].shape
    s = ctx_ops[0].shape[2]
    sb = min(SCAN_BLOCK, t)
    per_block, nblk, n_ctx = sb // DN_CHUNK, t // sb, s // DN_CHUNK
    blk = lambda d: (lambda i, j: (i, d, j if d == 0 else nblk - 1 - j, 0))
    seq_c = lambda d: pl.BlockSpec((None, None, s, vw), lambda i, j: (i, d, 0, 0))
    dl_c = lambda d: pl.BlockSpec((None, None, n_ctx, DN_HEADS, LANES), lambda i, j: (i, d, 0, 0, 0))
    seq_l = lambda d: pl.BlockSpec((None, None, sb, vw), blk(d))
    dl_l = lambda d: pl.BlockSpec((None, None, per_block, DN_HEADS, LANES), lambda i, j: blk(d)(i, j) + (0,))
    ctx_specs = lambda d: [seq_c(d), seq_c(d), seq_c(d), dl_c(d)]
    lat_specs = lambda d: [seq_l(d), seq_l(d), seq_l(d), dl_l(d), seq_l(d), seq_l(d)]
    out_f = pl.BlockSpec((None, sb, vw), lambda i, j: (i, j, 0))
    out_b = pl.BlockSpec((None, sb, vw), lambda i, j: (i, nblk - 1 - j, 0))
    out_shape = jax.ShapeDtypeStruct((b, t, vw), BF16)
    return pl.pallas_call(
        functools.partial(_dn_scan_body, n_ctx=n_ctx, per_block=per_block),
        grid=(b, nblk),
        in_specs=ctx_specs(0) + ctx_specs(1) + lat_specs(0) + lat_specs(1),
        out_specs=[out_f, out_b],
        out_shape=[out_shape, out_shape],
        scratch_shapes=[pltpu.VMEM((2, DN_HEADS, DN_DK, DN_DV), F32)],
        compiler_params=_params(2),
        name="deltanet_scan",
    )(*ctx_ops, *ctx_ops, *lat_ops, *lat_ops)


def _merge_ffn_body(x_ref, om_ref, of_ref, ob_ref, z_ref, gate_ref, gt1_ref, sh2_ref, sc2_ref, gt2_ref,
                    gdn_ref, woa_ref, wod_ref, wout_ref, gpost_ref, gpre2_ref, w1_ref, w2_ref, gpost2_ref, out_ref):
    odn = of_ref[...].astype(F32) + ob_ref[...].astype(F32)
    parts = []
    for h in range(DN_HEADS):
        hs = slice(h * DN_DV, (h + 1) * DN_DV)
        parts.append(_rms(odn[:, hs], gdn_ref[...]) * _silu(z_ref[:, hs].astype(F32)))
    o_dn = jnp.concatenate(parts, axis=1).astype(BF16)
    ya = _mm(om_ref[...], woa_ref[...])
    yb = _mm(o_dn, wod_ref[...])
    mix = gate_ref[:, :D_MODEL].astype(F32) * ya + gate_ref[:, D_MODEL:].astype(F32) * yb
    y = _mm(mix.astype(BF16), wout_ref[...])
    x1 = x_ref[...] + gt1_ref[...] * _rms(y, gpost_ref[...])
    u2 = (_rms(x1, gpre2_ref[...]) * (1.0 + sc2_ref[...]) + sh2_ref[...]).astype(BF16)
    acc = jnp.zeros(x1.shape, F32)
    for j in range(FF_HIDDEN // FF_TILE):
        a = _mm(u2, w1_ref[:, j * FF_TILE:(j + 1) * FF_TILE])
        up = _mm(u2, w1_ref[:, FF_HIDDEN + j * FF_TILE:FF_HIDDEN + (j + 1) * FF_TILE])
        acc = acc + _mm((_silu(a) * up).astype(BF16), w2_ref[j * FF_TILE:(j + 1) * FF_TILE, :])
    out_ref[...] = x1 + gt2_ref[...] * _rms(acc, gpost2_ref[...])


def _merge_ffn(x2, seq, om, o_f, o_b, z, gates, mod3, gdn, woa, wod, wout, gpost, gpre2, w1, w2, gpost2):
    n, d = x2.shape
    tm = min(ROW_TILE, seq)
    tiles = seq // tm
    row = lambda i: (i, 0)
    modspec = lambda k: pl.BlockSpec((None, 1, d), lambda i: (i // tiles, 0, k))
    dn_spec = pl.BlockSpec((tm, DN_VW), row)
    return pl.pallas_call(
        _merge_ffn_body,
        grid=(n // tm,),
        in_specs=[pl.BlockSpec((tm, d), row), pl.BlockSpec((tm, om.shape[1]), row), dn_spec, dn_spec,
                  pl.BlockSpec((tm, DN_VW), row), pl.BlockSpec((tm, 2 * d), row),
                  modspec(2), modspec(3), modspec(4), modspec(5),
                  _const_spec(gdn.shape), _const_spec(woa.shape), _const_spec(wod.shape), _const_spec(wout.shape),
                  _const_spec(gpost.shape), _const_spec(gpre2.shape), _const_spec(w1.shape), _const_spec(w2.shape),
                  _const_spec(gpost2.shape)],
        out_specs=pl.BlockSpec((tm, d), row),
        out_shape=jax.ShapeDtypeStruct((n, d), F32),
        compiler_params=_params(1),
        name="merge_ffn",
    )(x2, om, o_f, o_b, z, gates, mod3, mod3, mod3, mod3, gdn, woa, wod, wout, gpost, gpre2, w1, w2, gpost2)


def _rot_half(w):
    n = MLA_ROPE // 4
    return jnp.concatenate([-w[..., n:2 * n], w[..., 0:n], -w[..., 3 * n:4 * n], w[..., 2 * n:3 * n]], axis=-1)


def _rope_tables(rows):
    row = jnp.repeat(jnp.arange(rows, dtype=F32), GRID_W)
    col = jnp.tile(jnp.arange(GRID_W, dtype=F32), rows)
    n = MLA_ROPE // 4
    inv = ROPE_BASE ** (-jnp.arange(n, dtype=F32) / n)
    ar, ac = row[:, None] * inv, col[:, None] * inv
    cos = jnp.concatenate([jnp.cos(ar), jnp.cos(ar), jnp.cos(ac), jnp.cos(ac)], axis=-1)
    sin = jnp.concatenate([jnp.sin(ar), jnp.sin(ar), jnp.sin(ac), jnp.sin(ac)], axis=-1)
    return cos, sin


def _pad_tab(nope_val, rope, t):
    return jnp.concatenate([jnp.full((t, MLA_NOPE), nope_val, F32), rope,
                            jnp.zeros((t, LANES - MLA_NOPE - MLA_ROPE), F32)], axis=-1)


def kernel(x, c, ctx, c_ctx, w_mod, b_mod, g_pre_mix, g_post_mix, g_pre_ffn, g_post_ffn, w_in, g_q_lora, w_uq,
           g_kv_lora, w_ukv, w_o_mla, conv_qkv, a_log, dt_bias, g_dn_out, w_o_dn, w_out, w_ffn_in, w_ffn_out):
    B, T, D = x.shape
    S = ctx.shape[1]
    assert w_mod.shape[0] == 1 and D == D_MODEL
    assert T % max(ROW_TILE, Q_TILE, SCAN_BLOCK) == 0 and S % (PREP_CHUNKS * DN_CHUNK) == 0 and T % GRID_W == 0
    L = 0
    row2 = lambda a: a[L].reshape(1, -1)

    mod_rows = -(-(B + 1) // 8) * 8
    cc = jnp.concatenate([c, c_ctx[None, :], jnp.zeros((mod_rows - B - 1, D), F32)], axis=0)
    mod3 = _modulation(cc, w_mod[L], b_mod[L]).reshape(mod_rows, 1, 6 * D)

    wi = w_in[L]
    o_cq, o_ckv, o_kr = 0, Q_LORA, Q_LORA + KV_LORA
    o_qkv = o_kr + MLA_ROPE
    o_z = o_qkv + 2 * DN_QK + DN_VW
    o_beta = o_z + DN_VW
    o_gate = o_beta + 4 * DN_HEADS
    w_kr = wi[:, o_kr:o_qkv]
    zc = lambda n: jnp.zeros((D, n), F32)
    w_pack = jnp.concatenate([
        wi[:, o_cq:o_kr],
        wi[:, o_beta:o_gate], zc(MLA_NOPE - 4 * DN_HEADS), w_kr, zc(LANES - MLA_NOPE - MLA_ROPE),
        zc(MLA_NOPE), _rot_half(w_kr), zc(LANES - MLA_NOPE - MLA_ROPE),
        wi[:, o_qkv:o_beta], wi[:, o_gate:]], axis=1).astype(BF16)
    assert w_pack.shape[1] == C_END

    dq = MLA_NOPE + MLA_ROPE
    wq = w_uq[L].reshape(Q_LORA, MLA_HEADS, dq)
    zq = jnp.zeros((Q_LORA, MLA_HEADS, LANES - dq), F32)
    wq1 = jnp.concatenate([wq, zq], axis=-1).reshape(Q_LORA, MLA_HEADS * LANES).astype(BF16)
    wq2 = jnp.concatenate([jnp.zeros((Q_LORA, MLA_HEADS, MLA_NOPE), F32), _rot_half(wq[..., MLA_NOPE:]), zq],
                          axis=-1).reshape(Q_LORA, MLA_HEADS * LANES).astype(BF16)
    wkv = w_ukv[L].reshape(KV_LORA, MLA_HEADS, MLA_NOPE + MLA_V)
    wk = jnp.concatenate([wkv[..., :MLA_NOPE], jnp.zeros((KV_LORA, MLA_HEADS, LANES - MLA_NOPE), F32)],
                         axis=-1).reshape(KV_LORA, MLA_HEADS * LANES).astype(BF16)
    wv = wkv[..., MLA_NOPE:].reshape(KV_LORA, MLA_HEADS * MLA_V).astype(BF16)

    cos, sin = _rope_tables(T // GRID_W)
    qs = dq ** -0.5 * LOG2E
    tabs_lat = (_pad_tab(qs, cos * qs, T), _pad_tab(0.0, sin * qs, T), _pad_tab(0.0, cos, T), _pad_tab(0.0, sin, T))
    ones = jnp.ones((S, MLA_ROPE), F32)
    no_rot = _pad_tab(0.0, 0.0 * ones, S)
    tabs_ctx = (no_rot, no_rot, _pad_tab(0.0, ones, S), no_rot)

    x2 = x.reshape(B * T, D)
    c2 = ctx.reshape(B * S, D)
    gpre, gq, gkv = row2(g_pre_mix), row2(g_q_lora), row2(g_kv_lora)
    q, k, v, qkv, ba, z, gates = _input_projection(
        x2, mod3, T, T, w_pack, gpre, gq, wq1, wq2, gkv, wk, wv, tabs_lat, True)
    k_c, v_c, qkv_c, ba_c = _input_projection(
        c2, mod3[B:B + 1], B * S, S, w_pack[:, :C_CTX_END], gpre, gq, wq1, wq2, gkv, wk, wv, tabs_ctx, False)

    r3 = lambda a, n: a.reshape(B, n, a.shape[-1])
    o_mla = _attention(r3(q, T), r3(k, T), r3(k_c, S), r3(v, T), r3(v_c, S))

    lane_pad = lambda a, fill: jnp.pad(a.astype(F32).reshape(1, -1), ((0, 0), (8, LANES - 16)), constant_values=fill)
    alog = lane_pad(a_log[L], -1e30)
    dtb = lane_pad(dt_bias[L], 0.0)
    cw = conv_qkv[L].astype(F32)
    ctx_ops = _dn_prepare(r3(qkv_c, S), r3(ba_c, S), cw, alog, dtb, False)
    lat_ops = _dn_prepare(r3(qkv, T), r3(ba, T), cw, alog, dtb, True)
    o_f, o_b = _dn_scan(ctx_ops, lat_ops)

    out = _merge_ffn(x2, T, o_mla.reshape(B * T, -1), o_f.reshape(B * T, -1), o_b.reshape(B * T, -1), z, gates, mod3,
                     row2(g_dn_out),
                     w_o_mla[L].astype(BF16), w_o_dn[L].astype(BF16), w_out[L].astype(BF16), row2(g_post_mix),
                     row2(g_pre_ffn), w_ffn_in[L].astype(BF16), w_ffn_out[L].astype(BF16), row2(g_post_ffn))
    return out.reshape(B, T, D)
```

```python
import functools
import math

import jax
import jax.numpy as jnp
from jax import lax
from jax.experimental import pallas as pl
from jax.experimental.pallas import tpu as pltpu

F32 = jnp.float32
BF16 = jnp.bfloat16

D_MODEL = 1024
GRID_W = 64
MLA_HEADS = 8
MLA_NOPE = 64
MLA_ROPE = 32
MLA_V = 64
Q_LORA = 512
KV_LORA = 256
ROPE_BASE = 10000.0
DN_HEADS = 4
DN_DK = 128
DN_DV = 128
DN_QK = DN_HEADS * DN_DK
DN_VW = DN_HEADS * DN_DV
FF_HIDDEN = 2816
EPS = 1e-6

LANES = 128
DN_CHUNK = 128
ROW_TILE = 512
Q_TILE = 512
V_GROUP = 4
SCAN_BLOCK = 1024
PREP_CHUNKS = 2
FF_TILE = 256
PACK_ROWS = 16
VMEM_LIMIT = 56 * 1024 * 1024
LOG2E = math.log2(math.e)

C_CQ = 0
C_CKV = C_CQ + Q_LORA
C_M1 = C_CKV + KV_LORA
C_M2 = C_M1 + LANES
C_QKV = C_M2 + LANES
C_Z = C_QKV + 2 * DN_QK + DN_VW
C_GATE = C_Z + DN_VW
C_END = C_GATE + 2 * D_MODEL
C_CTX_END = C_Z


def _mm(a, b):
    return jnp.dot(a, b, preferred_element_type=F32)


def _mm_nt(a, b):
    return lax.dot_general(a, b, (((1,), (1,)), ((), ())), preferred_element_type=F32)


def _sigmoid(x):
    return 1.0 / (1.0 + jnp.exp(-x))


def _silu(x):
    return x * _sigmoid(x)


def _rms(x, g):
    return x * lax.rsqrt(jnp.mean(x * x, axis=-1, keepdims=True) + EPS) * g


def _params(n_axes):
    return pltpu.CompilerParams(dimension_semantics=("arbitrary",) * n_axes, vmem_limit_bytes=VMEM_LIMIT)


def _const_spec(shape):
    nd = len(shape)
    return pl.BlockSpec(shape, lambda *_: (0,) * nd, pipeline_mode=pl.Buffered(1))


def _mod_body(c_ref, w_ref, b_ref, o_ref):
    s = _silu(c_ref[...])
    o_ref[...] = jnp.dot(s, w_ref[...], preferred_element_type=F32, precision=lax.Precision.HIGHEST) + b_ref[...]


def _modulation(cc, w_mod, b_mod):
    rows, d = cc.shape
    n = w_mod.shape[1]
    tn = 1024
    return pl.pallas_call(
        _mod_body,
        grid=(n // tn,),
        in_specs=[pl.BlockSpec((rows, d), lambda j: (0, 0)),
                  pl.BlockSpec((d, tn), lambda j: (0, j)),
                  pl.BlockSpec((1, tn), lambda j: (0, j))],
        out_specs=pl.BlockSpec((rows, tn), lambda j: (0, j)),
        out_shape=jax.ShapeDtypeStruct((rows, n), F32),
        compiler_params=_params(1),
        name="modulation",
    )(cc, w_mod, b_mod.reshape(1, n))


def _inproj_body(x_ref, sh_ref, sc_ref, gpre_ref, w_ref, gq_ref, wq1_ref, wq2_ref, gkv_ref, wk_ref, wv_ref,
                 cq_tab, sq_tab, ck_tab, sk_tab, *outs, full):
    if full:
        q_ref, k_ref, v_ref, qkv_ref, ba_ref, z_ref, gate_ref = outs
    else:
        k_ref, v_ref, qkv_ref, ba_ref = outs
    u = (_rms(x_ref[...], gpre_ref[...]) * (1.0 + sc_ref[...]) + sh_ref[...]).astype(BF16)

    if full:
        nq = _rms(_mm(u, w_ref[:, C_CQ:C_CKV]), gq_ref[...]).astype(BF16)
        qa = _mm(nq, wq1_ref[...])
        qb = _mm(nq, wq2_ref[...])
        cq = cq_tab[...]
        sq = sq_tab[...]
        for h in range(MLA_HEADS):
            sl = slice(h * LANES, (h + 1) * LANES)
            q_ref[:, sl] = (qa[:, sl] * cq + qb[:, sl] * sq).astype(BF16)

    nkv = _rms(_mm(u, w_ref[:, C_CKV:C_M1]), gkv_ref[...]).astype(BF16)
    m1 = _mm(u, w_ref[:, C_M1:C_M2])
    m2 = _mm(u, w_ref[:, C_M2:C_QKV])
    ba_ref[...] = m1
    k_rope = m1 * ck_tab[...] + m2 * sk_tab[...]
    kn = _mm(nkv, wk_ref[...])
    for h in range(MLA_HEADS):
        sl = slice(h * LANES, (h + 1) * LANES)
        k_ref[:, sl] = (kn[:, sl] + k_rope).astype(BF16)
    v_ref[...] = _mm(nkv, wv_ref[...]).astype(BF16)
    qkv_ref[...] = _mm(u, w_ref[:, C_QKV:C_Z]).astype(BF16)
    if full:
        z_ref[...] = _mm(u, w_ref[:, C_Z:C_GATE]).astype(BF16)
        gate_ref[...] = _sigmoid(_mm(u, w_ref[:, C_GATE:C_END])).astype(BF16)


def _input_projection(x2, mod3, mod_period, tab_period, w, gpre, gq, wq1, wq2, gkv, wk, wv, tabs, full):
    n, d = x2.shape
    tm = min(ROW_TILE, tab_period)
    mod_tiles = mod_period // tm
    tab_tiles = tab_period // tm
    row = lambda i: (i, 0)
    tab = lambda i: (i % tab_tiles, 0)
    in_specs = [
        pl.BlockSpec((tm, d), row),
        pl.BlockSpec((None, 1, d), lambda i: (i // mod_tiles, 0, 0)),
        pl.BlockSpec((None, 1, d), lambda i: (i // mod_tiles, 0, 1)),
        _const_spec(gpre.shape), _const_spec(w.shape), _const_spec(gq.shape), _const_spec(wq1.shape),
        _const_spec(wq2.shape), _const_spec(gkv.shape), _const_spec(wk.shape), _const_spec(wv.shape),
        pl.BlockSpec((tm, LANES), tab), pl.BlockSpec((tm, LANES), tab), pl.BlockSpec((tm, LANES), tab),
        pl.BlockSpec((tm, LANES), tab),
    ]
    kw = MLA_HEADS * LANES
    vw = MLA_HEADS * MLA_V
    qkvw = 2 * DN_QK + DN_VW
    outs = [(kw, BF16), (vw, BF16), (qkvw, BF16), (LANES, F32)]
    if full:
        outs = [(kw, BF16)] + outs + [(DN_VW, BF16), (2 * D_MODEL, BF16)]
    return pl.pallas_call(
        functools.partial(_inproj_body, full=full),
        grid=(n // tm,),
        in_specs=in_specs,
        out_specs=[pl.BlockSpec((tm, c), row) for c, _ in outs],
        out_shape=[jax.ShapeDtypeStruct((n, c), t) for c, t in outs],
        compiler_params=_params(1),
        name="input_projection" if full else "input_projection_ctx",
    )(x2, mod3, mod3, gpre, w, gq, wq1, wq2, gkv, wk, wv, *tabs)


def _attn_body(q_ref, kl_ref, kc_ref, vl_ref, vc_ref, o_ref):
    for h in range(MLA_HEADS):
        sl = slice(h * LANES, (h + 1) * LANES)
        sv = slice(h * MLA_V, (h + 1) * MLA_V)
        qh = q_ref[:, sl]
        s1 = _mm_nt(qh, kl_ref[:, sl])
        s2 = _mm_nt(qh, kc_ref[:, sl])
        m = jnp.maximum(jnp.max(s1, axis=-1, keepdims=True), jnp.max(s2, axis=-1, keepdims=True))
        p1 = jnp.exp2(s1 - m)
        p2 = jnp.exp2(s2 - m)
        l = jnp.sum(p1, axis=-1, keepdims=True) + jnp.sum(p2, axis=-1, keepdims=True)
        g0 = h // V_GROUP * V_GROUP * MLA_V
        sg = slice(g0, g0 + V_GROUP * MLA_V)
        og = _mm(p1.astype(BF16), vl_ref[:, sg]) + _mm(p2.astype(BF16), vc_ref[:, sg])
        o = og[:, h % V_GROUP * MLA_V:(h % V_GROUP + 1) * MLA_V]
        o_ref[:, sv] = (o / l).astype(BF16)


def _attention(q, kl, kc, vl, vc):
    b, t, kw = q.shape
    s = kc.shape[1]
    vw = vl.shape[2]
    tq = min(Q_TILE, t)
    return pl.pallas_call(
        _attn_body,
        grid=(b, t // tq),
        in_specs=[pl.BlockSpec((None, tq, kw), lambda i, j: (i, j, 0)),
                  pl.BlockSpec((None, t, kw), lambda i, j: (i, 0, 0)),
                  pl.BlockSpec((None, s, kw), lambda i, j: (i, 0, 0)),
                  pl.BlockSpec((None, t, vw), lambda i, j: (i, 0, 0)),
                  pl.BlockSpec((None, s, vw), lambda i, j: (i, 0, 0))],
        out_specs=pl.BlockSpec((None, tq, vw), lambda i, j: (i, j, 0)),
        out_shape=jax.ShapeDtypeStruct((b, t, vw), BF16),
        compiler_params=_params(2),
        name="attention",
    )(q, kl, kc, vl, vc)


def _dn_prep_body(x_ref, xp_ref, xn_ref, ba_ref, cw_ref, alog_ref, dtb_ref, *outs, nchunks, emit):
    c = pl.program_id(1)
    C = DN_CHUNK
    nsub = x_ref.shape[0] // C
    has_prev = jnp.where(c > 0, 1.0, 0.0)
    has_next = jnp.where(c < nchunks // nsub - 1, 1.0, 0.0)
    rowi = lax.broadcasted_iota(jnp.int32, (C, C), 0)
    coli = lax.broadcasted_iota(jnp.int32, (C, C), 1)

    def l2n(a):
        return a * lax.rsqrt(jnp.sum(a * a, axis=-1, keepdims=True) + EPS)

    systems = []
    for j in range(nsub):
        _dn_prep_chunk(j, nsub, x_ref, xp_ref, xn_ref, ba_ref, cw_ref, alog_ref, dtb_ref, outs, emit, has_prev,
                       has_next, rowi, coli, l2n, systems)
    u_ref, w_ref = outs[0], outs[1]

    eye = jnp.where(rowi == coli, 1.0, 0.0)
    ts = [eye + jnp.where((rowi >> 1) == (coli >> 1), sysm[3], 0.0) for sysm in systems]
    nbs = [sysm[3].astype(BF16) for sysm in systems]
    zero = jnp.zeros((C, C), BF16)
    for lg in range(1, int(math.log2(C))):
        cross = ((rowi >> (lg + 1)) == (coli >> (lg + 1))) & ((rowi >> lg) != (coli >> lg))
        tbs = [t.astype(BF16) for t in ts]
        ms = [_mm(tb, jnp.where(cross, nb, zero)) for tb, nb in zip(tbs, nbs)]
        ts = [t + _mm(m.astype(BF16), tb) for t, m, tb in zip(ts, ms, tbs)]
    for (j, d, hs, _, rhs), t in zip(systems, ts):
        rows = slice(j * C, (j + 1) * C)
        uw = _mm(t.astype(BF16), rhs.astype(BF16))
        u_ref[d, rows, hs] = uw[:, :LANES].astype(BF16)
        w_ref[d, rows, hs] = uw[:, LANES:].astype(BF16)


def _dn_prep_chunk(j, nsub, x_ref, xp_ref, xn_ref, ba_ref, cw_ref, alog_ref, dtb_ref, outs, emit, has_prev, has_next,
                   rowi, coli, l2n, systems):
    if emit:
        u_ref, w_ref, kdt_ref, dl_ref, qg_ref, qkm_ref = outs
    else:
        u_ref, w_ref, kdt_ref, dl_ref = outs
    C = DN_CHUNK
    rows = slice(j * C, (j + 1) * C)

    def conv_slab(s):
        sl = slice(s * LANES, (s + 1) * LANES)
        x = x_ref[rows, sl].astype(F32)
        if j == 0:
            prev = xp_ref[PACK_ROWS - 1:PACK_ROWS, sl].astype(F32) * has_prev
        else:
            prev = x_ref[j * C - 1:j * C, sl].astype(F32)
        if j == nsub - 1:
            nxt = xn_ref[0:1, sl].astype(F32) * has_next
        else:
            nxt = x_ref[(j + 1) * C:(j + 1) * C + 1, sl].astype(F32)
        xd = jnp.where(rowi == 0, prev, pltpu.roll(x, 1, 0))
        xu = jnp.where(rowi == C - 1, nxt, pltpu.roll(x, C - 1, 0))
        return _silu(xd * cw_ref[0:1, sl] + x * cw_ref[1:2, sl] + xu * cw_ref[2:3, sl])

    bg = ba_ref[rows, :]
    beta = _sigmoid(bg)
    xs = bg + dtb_ref[...]
    softplus = jnp.maximum(xs, 0.0) + jnp.log(1.0 + jnp.exp(-jnp.abs(xs)))
    g = -jnp.exp(alog_ref[...]) * softplus
    gc = g
    sft = 1
    while sft < C:
        gc = gc + jnp.where(rowi >= sft, pltpu.roll(gc, sft, 0), 0.0)
        sft *= 2
    gtot = gc[C - 1:C, :]
    gcr = gtot - gc + g
    gc_t = gc.T
    gcr_t = gcr.T
    e_gc = jnp.exp(gc)
    e_gcr = jnp.exp(gcr)
    e_rem = jnp.exp(gtot - gc)
    e_rem_r = jnp.exp(gtot - gcr)
    dl_all = jnp.exp(jnp.broadcast_to(gc_t[8:16, C - 1:C], (8, LANES)))
    scale = DN_DK ** -0.5

    for h in range(DN_HEADS):
        hs = slice(h * LANES, (h + 1) * LANES)
        qh = l2n(conv_slab(h)) * scale
        kh = l2n(conv_slab(DN_HEADS + h))
        vh = conv_slab(2 * DN_HEADS + h)
        khb = kh.astype(BF16)
        kk = _mm_nt(khb, khb)
        if emit:
            qk = _mm_nt(qh.astype(BF16), khb)
        for d in range(2):
            lane = 8 + 4 * d + h
            gsrc, gsrc_t, esrc, erem = (gc, gc_t, e_gc, e_rem) if d == 0 else (gcr, gcr_t, e_gcr, e_rem_r)
            gcol = gsrc[:, lane:lane + 1]
            grow = gsrc_t[lane:lane + 1, :]
            bcol = beta[:, 4 * d + h:4 * d + h + 1]
            incl = (rowi >= coli) if d == 0 else (rowi <= coli)
            strict = (rowi > coli) if d == 0 else (rowi < coli)
            decay = jnp.exp(jnp.where(incl, gcol - grow, -1e30))
            ecol = esrc[:, lane:lane + 1]
            systems.append((j, d, hs, jnp.where(strict, -(kk * bcol) * decay, 0.0),
                            jnp.concatenate([vh * bcol, kh * (bcol * ecol)], axis=1)))
            kdt_ref[d, rows, hs] = (kh * erem[:, lane:lane + 1]).T.astype(BF16)
            dl_ref[d, j, h:h + 1, :] = dl_all[4 * d + h:4 * d + h + 1, :]
            if emit:
                qg_ref[d, rows, hs] = (qh * ecol).astype(BF16)
                qkm_ref[d, rows, hs] = jnp.where(incl, qk * decay, 0.0).astype(BF16)


def _dn_prepare(qkv, ba, conv_w, alog, dtb, emit):
    b, t, cw = qkv.shape
    nchunks = t // DN_CHUNK
    C = PREP_CHUNKS * DN_CHUNK
    per = C // PACK_ROWS
    last = t // PACK_ROWS - 1
    seq_out = pl.BlockSpec((None, 2, C, DN_VW), lambda i, c: (i, 0, c, 0))
    seq_shape = jax.ShapeDtypeStruct((b, 2, t, DN_VW), BF16)
    out_specs = [seq_out, seq_out, seq_out,
                 pl.BlockSpec((None, 2, PREP_CHUNKS, DN_HEADS, LANES), lambda i, c: (i, 0, c, 0, 0))]
    out_shape = [seq_shape, seq_shape, seq_shape, jax.ShapeDtypeStruct((b, 2, nchunks, DN_HEADS, LANES), F32)]
    if emit:
        out_specs += [seq_out, seq_out]
        out_shape += [seq_shape, seq_shape]
    return pl.pallas_call(
        functools.partial(_dn_prep_body, nchunks=nchunks, emit=emit),
        grid=(b, t // C),
        in_specs=[pl.BlockSpec((None, C, cw), lambda i, c: (i, c, 0)),
                  pl.BlockSpec((None, PACK_ROWS, cw), lambda i, c: (i, jnp.maximum(c * per - 1, 0), 0)),
                  pl.BlockSpec((None, PACK_ROWS, cw), lambda i, c: (i, jnp.minimum((c + 1) * per, last), 0)),
                  pl.BlockSpec((None, C, LANES), lambda i, c: (i, c, 0)),
                  _const_spec(conv_w.shape), _const_spec(alog.shape), _const_spec(dtb.shape)],
        out_specs=out_specs,
        out_shape=out_shape,
        compiler_params=_params(2),
        name="deltanet_prepare" if emit else "deltanet_prepare_ctx",
    )(qkv, qkv, qkv, ba, conv_w, alog, dtb)


def _dn_scan_body(*refs, n_ctx, per_block):
    ctx_f, ctx_b = refs[0:4], refs[4:8]
    lat_f, lat_b = refs[8:14], refs[14:20]
    of_ref, ob_ref, s_ref = refs[20:23]
    C = DN_CHUNK
    heads = range(DN_HEADS)
    hs = [slice(h * LANES, (h + 1) * LANES) for h in heads]

    def step(srcs):
        chains = [(d, lc, ops, o_ref, h) for d, lc, ops, o_ref in srcs for h in heads]
        rows = lambda lc: slice(lc * C, (lc + 1) * C)
        ss = [s_ref[d, h] for d, _, _, _, h in chains]
        sbs = [s.astype(BF16) for s in ss]
        ws = [_mm(ops[1][rows(lc), hs[h]], sb) for (_, lc, ops, _, h), sb in zip(chains, sbs)]
        os = [_mm(ops[4][rows(lc), hs[h]], sb) if o_ref is not None else None
              for (_, lc, ops, o_ref, h), sb in zip(chains, sbs)]
        vnbs = [(ops[0][rows(lc), hs[h]].astype(F32) - w).astype(BF16) for (_, lc, ops, _, h), w in zip(chains, ws)]
        for (d, lc, ops, _, h), s, vnb in zip(chains, ss, vnbs):
            s_ref[d, h] = s * ops[3][lc, h:h + 1, :] + _mm(ops[2][rows(lc), hs[h]], vnb)
        for (_, lc, ops, o_ref, h), o, vnb in zip(chains, os, vnbs):
            if o_ref is not None:
                o_ref[rows(lc), hs[h]] = (o + _mm(ops[5][rows(lc), hs[h]], vnb)).astype(BF16)

    @pl.when(pl.program_id(1) == 0)
    def _():
        s_ref[...] = jnp.zeros_like(s_ref)
        for c in range(n_ctx):
            step([(0, c, ctx_f, None), (1, n_ctx - 1 - c, ctx_b, None)])

    for c in range(per_block):
        step([(0, c, lat_f, of_ref), (1, per_block - 1 - c, lat_b, ob_ref)])


def _dn_scan(ctx_ops, lat_ops):
    b, _, t, vw = lat_ops[0].shape
    s = ctx_ops[0].shape[2]
    sb = min(SCAN_BLOCK, t)
    per_block, nblk, n_ctx = sb // DN_CHUNK, t // sb, s // DN_CHUNK
    blk = lambda d: (lambda i, j: (i, d, j if d == 0 else nblk - 1 - j, 0))
    seq_c = lambda d: pl.BlockSpec((None, None, s, vw), lambda i, j: (i, d, 0, 0))
    dl_c = lambda d: pl.BlockSpec((None, None, n_ctx, DN_HEADS, LANES), lambda i, j: (i, d, 0, 0, 0))
    seq_l = lambda d: pl.BlockSpec((None, None, sb, vw), blk(d))
    dl_l = lambda d: pl.BlockSpec((None, None, per_block, DN_HEADS, LANES), lambda i, j: blk(d)(i, j) + (0,))
    ctx_specs = lambda d: [seq_c(d), seq_c(d), seq_c(d), dl_c(d)]
    lat_specs = lambda d: [seq_l(d), seq_l(d), seq_l(d), dl_l(d), seq_l(d), seq_l(d)]
    out_f = pl.BlockSpec((None, sb, vw), lambda i, j: (i, j, 0))
    out_b = pl.BlockSpec((None, sb, vw), lambda i, j: (i, nblk - 1 - j, 0))
    out_shape = jax.ShapeDtypeStruct((b, t, vw), BF16)
    return pl.pallas_call(
        functools.partial(_dn_scan_body, n_ctx=n_ctx, per_block=per_block),
        grid=(b, nblk),
        in_specs=ctx_specs(0) + ctx_specs(1) + lat_specs(0) + lat_specs(1),
        out_specs=[out_f, out_b],
        out_shape=[out_shape, out_shape],
        scratch_shapes=[pltpu.VMEM((2, DN_HEADS, DN_DK, DN_DV), F32)],
        compiler_params=_params(2),
        name="deltanet_scan",
    )(*ctx_ops, *ctx_ops, *lat_ops, *lat_ops)


def _merge_ffn_body(x_ref, om_ref, of_ref, ob_ref, z_ref, gate_ref, gt1_ref, sh2_ref, sc2_ref, gt2_ref,
                    gdn_ref, woa_ref, wod_ref, wout_ref, gpost_ref, gpre2_ref, w1_ref, w2_ref, gpost2_ref, out_ref):
    odn = of_ref[...].astype(F32) + ob_ref[...].astype(F32)
    parts = []
    for h in range(DN_HEADS):
        hs = slice(h * DN_DV, (h + 1) * DN_DV)
        parts.append(_rms(odn[:, hs], gdn_ref[...]) * _silu(z_ref[:, hs].astype(F32)))
    o_dn = jnp.concatenate(parts, axis=1).astype(BF16)
    ya = _mm(om_ref[...], woa_ref[...])
    yb = _mm(o_dn, wod_ref[...])
    mix = gate_ref[:, :D_MODEL].astype(F32) * ya + gate_ref[:, D_MODEL:].astype(F32) * yb
    y = _mm(mix.astype(BF16), wout_ref[...])
    x1 = x_ref[...] + gt1_ref[...] * _rms(y, gpost_ref[...])
    u2 = (_rms(x1, gpre2_ref[...]) * (1.0 + sc2_ref[...]) + sh2_ref[...]).astype(BF16)
    acc = jnp.zeros(x1.shape, F32)
    for j in range(FF_HIDDEN // FF_TILE):
        a = _mm(u2, w1_ref[:, j * FF_TILE:(j + 1) * FF_TILE])
        up = _mm(u2, w1_ref[:, FF_HIDDEN + j * FF_TILE:FF_HIDDEN + (j + 1) * FF_TILE])
        acc = acc + _mm((_silu(a) * up).astype(BF16), w2_ref[j * FF_TILE:(j + 1) * FF_TILE, :])
    out_ref[...] = x1 + gt2_ref[...] * _rms(acc, gpost2_ref[...])


def _merge_ffn(x2, seq, om, o_f, o_b, z, gates, mod3, gdn, woa, wod, wout, gpost, gpre2, w1, w2, gpost2):
    n, d = x2.shape
    tm = min(ROW_TILE, seq)
    tiles = seq // tm
    row = lambda i: (i, 0)
    modspec = lambda k: pl.BlockSpec((None, 1, d), lambda i: (i // tiles, 0, k))
    dn_spec = pl.BlockSpec((tm, DN_VW), row)
    return pl.pallas_call(
        _merge_ffn_body,
        grid=(n // tm,),
        in_specs=[pl.BlockSpec((tm, d), row), pl.BlockSpec((tm, om.shape[1]), row), dn_spec, dn_spec,
                  pl.BlockSpec((tm, DN_VW), row), pl.BlockSpec((tm, 2 * d), row),
                  modspec(2), modspec(3), modspec(4), modspec(5),
                  _const_spec(gdn.shape), _const_spec(woa.shape), _const_spec(wod.shape), _const_spec(wout.shape),
                  _const_spec(gpost.shape), _const_spec(gpre2.shape), _const_spec(w1.shape), _const_spec(w2.shape),
                  _const_spec(gpost2.shape)],
        out_specs=pl.BlockSpec((tm, d), row),
        out_shape=jax.ShapeDtypeStruct((n, d), F32),
        compiler_params=_params(1),
        name="merge_ffn",
    )(x2, om, o_f, o_b, z, gates, mod3, mod3, mod3, mod3, gdn, woa, wod, wout, gpost, gpre2, w1, w2, gpost2)


def _rot_half(w):
    n = MLA_ROPE // 4
    return jnp.concatenate([-w[..., n:2 * n], w[..., 0:n], -w[..., 3 * n:4 * n], w[..., 2 * n:3 * n]], axis=-1)


def _rope_tables(rows):
    row = jnp.repeat(jnp.arange(rows, dtype=F32), GRID_W)
    col = jnp.tile(jnp.arange(GRID_W, dtype=F32), rows)
    n = MLA_ROPE // 4
    inv = ROPE_BASE ** (-jnp.arange(n, dtype=F32) / n)
    ar, ac = row[:, None] * inv, col[:, None] * inv
    cos = jnp.concatenate([jnp.cos(ar), jnp.cos(ar), jnp.cos(ac), jnp.cos(ac)], axis=-1)
    sin = jnp.concatenate([jnp.sin(ar), jnp.sin(ar), jnp.sin(ac), jnp.sin(ac)], axis=-1)
    return cos, sin


def _pad_tab(nope_val, rope, t):
    return jnp.concatenate([jnp.full((t, MLA_NOPE), nope_val, F32), rope,
                            jnp.zeros((t, LANES - MLA_NOPE - MLA_ROPE), F32)], axis=-1)


def kernel(x, c, ctx, c_ctx, w_mod, b_mod, g_pre_mix, g_post_mix, g_pre_ffn, g_post_ffn, w_in, g_q_lora, w_uq,
           g_kv_lora, w_ukv, w_o_mla, conv_qkv, a_log, dt_bias, g_dn_out, w_o_dn, w_out, w_ffn_in, w_ffn_out):
    B, T, D = x.shape
    S = ctx.shape[1]
    assert w_mod.shape[0] == 1 and D == D_MODEL
    assert T % max(ROW_TILE, Q_TILE, SCAN_BLOCK) == 0 and S % (PREP_CHUNKS * DN_CHUNK) == 0 and T % GRID_W == 0
    L = 0
    row2 = lambda a: a[L].reshape(1, -1)

    mod_rows = -(-(B + 1) // 8) * 8
    cc = jnp.concatenate([c, c_ctx[None, :], jnp.zeros((mod_rows - B - 1, D), F32)], axis=0)
    mod3 = _modulation(cc, w_mod[L], b_mod[L]).reshape(mod_rows, 1, 6 * D)

    wi = w_in[L]
    o_cq, o_ckv, o_kr = 0, Q_LORA, Q_LORA + KV_LORA
    o_qkv = o_kr + MLA_ROPE
    o_z = o_qkv + 2 * DN_QK + DN_VW
    o_beta = o_z + DN_VW
    o_gate = o_beta + 4 * DN_HEADS
    w_kr = wi[:, o_kr:o_qkv]
    zc = lambda n: jnp.zeros((D, n), F32)
    w_pack = jnp.concatenate([
        wi[:, o_cq:o_kr],
        wi[:, o_beta:o_gate], zc(MLA_NOPE - 4 * DN_HEADS), w_kr, zc(LANES - MLA_NOPE - MLA_ROPE),
        zc(MLA_NOPE), _rot_half(w_kr), zc(LANES - MLA_NOPE - MLA_ROPE),
        wi[:, o_qkv:o_beta], wi[:, o_gate:]], axis=1).astype(BF16)
    assert w_pack.shape[1] == C_END

    dq = MLA_NOPE + MLA_ROPE
    wq = w_uq[L].reshape(Q_LORA, MLA_HEADS, dq)
    zq = jnp.zeros((Q_LORA, MLA_HEADS, LANES - dq), F32)
    wq1 = jnp.concatenate([wq, zq], axis=-1).reshape(Q_LORA, MLA_HEADS * LANES).astype(BF16)
    wq2 = jnp.concatenate([jnp.zeros((Q_LORA, MLA_HEADS, MLA_NOPE), F32), _rot_half(wq[..., MLA_NOPE:]), zq],
                          axis=-1).reshape(Q_LORA, MLA_HEADS * LANES).astype(BF16)
    wkv = w_ukv[L].reshape(KV_LORA, MLA_HEADS, MLA_NOPE + MLA_V)
    wk = jnp.concatenate([wkv[..., :MLA_NOPE], jnp.zeros((KV_LORA, MLA_HEADS, LANES - MLA_NOPE), F32)],
                         axis=-1).reshape(KV_LORA, MLA_HEADS * LANES).astype(BF16)
    wv = wkv[..., MLA_NOPE:].reshape(KV_LORA, MLA_HEADS * MLA_V).astype(BF16)

    cos, sin = _rope_tables(T // GRID_W)
    qs = dq ** -0.5 * LOG2E
    tabs_lat = (_pad_tab(qs, cos * qs, T), _pad_tab(0.0, sin * qs, T), _pad_tab(0.0, cos, T), _pad_tab(0.0, sin, T))
    ones = jnp.ones((S, MLA_ROPE), F32)
    no_rot = _pad_tab(0.0, 0.0 * ones, S)
    tabs_ctx = (no_rot, no_rot, _pad_tab(0.0, ones, S), no_rot)

    x2 = x.reshape(B * T, D)
    c2 = ctx.reshape(B * S, D)
    gpre, gq, gkv = row2(g_pre_mix), row2(g_q_lora), row2(g_kv_lora)
    q, k, v, qkv, ba, z, gates = _input_projection(
        x2, mod3, T, T, w_pack, gpre, gq, wq1, wq2, gkv, wk, wv, tabs_lat, True)
    k_c, v_c, qkv_c, ba_c = _input_projection(
        c2, mod3[B:B + 1], B * S, S, w_pack[:, :C_CTX_END], gpre, gq, wq1, wq2, gkv, wk, wv, tabs_ctx, False)

    r3 = lambda a, n: a.reshape(B, n, a.shape[-1])
    o_mla = _attention(r3(q, T), r3(k, T), r3(k_c, S), r3(v, T), r3(v_c, S))

    lane_pad = lambda a, fill: jnp.pad(a.astype(F32).reshape(1, -1), ((0, 0), (8, LANES - 16)), constant_values=fill)
    alog = lane_pad(a_log[L], -1e30)
    dtb = lane_pad(dt_bias[L], 0.0)
    cw = conv_qkv[L].astype(F32)
    ctx_ops = _dn_prepare(r3(qkv_c, S), r3(ba_c, S), cw, alog, dtb, False)
    lat_ops = _dn_prepare(r3(qkv, T), r3(ba, T), cw, alog, dtb, True)
    o_f, o_b = _dn_scan(ctx_ops, lat_ops)

    out = _merge_ffn(x2, T, o_mla.reshape(B * T, -1), o_f.reshape(B * T, -1), o_b.reshape(B * T, -1), z, gates, mod3,
                     row2(g_dn_out),
                     w_o_mla[L].astype(BF16), w_o_dn[L].astype(BF16), w_out[L].astype(BF16), row2(g_post_mix),
                     row2(g_pre_ffn), w_ffn_in[L].astype(BF16), w_ffn_out[L].astype(BF16), row2(g_post_ffn))
    return out.reshape(B, T, D)
```
